```python
import functools
import jax, jax.numpy as jnp
from jax import lax
import numpy as np

D_MODEL = 2048
BATCH = 8
SEQ = 2048
DEPTH = 1
DEC_BATCH = 32
DEC_SEQ = 64
PAST_LEN = 4096

CHUNK = 64
HEAD_DIM = 64
D_A = D_MODEL // 2
A_HEADS = D_A // HEAD_DIM
R_W = 64
R_A = 64
SHIFT_W = 3 * D_A + R_W + R_A
LNX_EPS = 64e-5
D_B = D_MODEL // 2
Q_HEADS = D_B // HEAD_DIM
KV_HEADS = 4
GROUP = Q_HEADS // KV_HEADS
KV_W = KV_HEADS * HEAD_DIM
WINDOW = 128
WIN_CHUNKS = WINDOW // CHUNK
CACHE_WIN = min(WINDOW, PAST_LEN)
IN_W = SHIFT_W + D_A + D_B + 2 * KV_W + D_B + 2 * D_MODEL
RMS_EPS = 1e-6
NEG_INF = -1e30

kernel_name = 'hybrid_rwkv7_swa_sink_streaming_step'


def rms_norm(x, g):
    x32 = x.astype(jnp.float32)
    y = x32 * lax.rsqrt(jnp.mean(x32 * x32, axis=-1, keepdims=True) + RMS_EPS)
    return (y * g.astype(jnp.float32)).astype(x.dtype)


def project(x, g_norm, w_in):
    h = rms_norm(x, g_norm)
    z = jnp.einsum('btd,de->bte', h, w_in)
    sizes = (SHIFT_W, D_A, D_B, KV_W, KV_W, D_B, D_MODEL, D_MODEL)
    offs = [int(o) for o in np.cumsum(sizes)[:-1]]
    return jnp.split(z, offs, axis=-1)


def rwkv7_branch(p, gate, shift_prev, wkv0, mu, w0, w_w_up, a0, w_a_up, k_k, k_a, r_k, lnx_w, lnx_b):
    bsz, t_len, _ = p.shape
    f32 = jnp.float32
    prev = jnp.concatenate([shift_prev[:, None, :].astype(p.dtype), p[:, :-1]], axis=1)
    xs = p + mu * (prev - p)
    r, k, v, wd, ad = jnp.split(xs, [D_A, 2 * D_A, 3 * D_A, 3 * D_A + R_W], axis=-1)
    w = -jax.nn.softplus(-(w0 + jnp.tanh(wd) @ w_w_up).astype(f32)) - 0.5
    decay = jnp.exp(-jnp.exp(w))
    a = jax.nn.sigmoid((a0 + ad @ w_a_up).astype(f32))
    heads = lambda u: u.astype(f32).reshape(bsz, t_len, A_HEADS, HEAD_DIM)
    r, k, v, decay, a = heads(r), heads(k), heads(v), heads(decay), heads(a)
    kk = k * k_k.astype(f32).reshape(A_HEADS, HEAD_DIM)
    kk = kk / jnp.maximum(jnp.sqrt(jnp.sum(kk * kk, axis=-1, keepdims=True)), 1e-12)
    k = k * (1.0 + (a - 1.0) * k_a.astype(f32).reshape(A_HEADS, HEAD_DIM))
    seq = tuple(jnp.moveaxis(u, 1, 0) for u in (r, decay, k, v, -kk, kk * a))

    def step(S, inp):
        r_t, w_t, k_t, v_t, a_t, b_t = inp
        sa = jnp.einsum('bhij,bhj->bhi', S, a_t)
        S = S * w_t[:, :, None, :] + sa[..., None] * b_t[:, :, None, :] + v_t[..., None] * k_t[:, :, None, :]
        return S, jnp.einsum('bhij,bhj->bhi', S, r_t)

    s_final, y = lax.scan(step, wkv0.astype(f32), seq)
    y = jnp.moveaxis(y, 0, 1)
    mean = jnp.mean(y, axis=-1, keepdims=True)
    var = jnp.mean(jnp.square(y - mean), axis=-1, keepdims=True)
    y = (y - mean) * lax.rsqrt(var + LNX_EPS) * lnx_w.astype(f32).reshape(A_HEADS, HEAD_DIM) \
        + lnx_b.astype(f32).reshape(A_HEADS, HEAD_DIM)
    y = y + jnp.sum(r * k * r_k.astype(f32), axis=-1, keepdims=True) * v
    y = y.reshape(bsz, t_len, D_A).astype(p.dtype) * jax.nn.silu(gate)
    return y, p[:, -1], s_final.astype(wkv0.dtype)


def sink_attention(q, k, v, q_pos, k_pos, sinks):
    s = jnp.einsum('...qhgd,...khd->...hgqk', q, k, preferred_element_type=jnp.float32) * (HEAD_DIM ** -0.5)
    dist = (q_pos[..., :, None] - k_pos[..., None, :]).astype(jnp.float32)
    dchunk = jnp.floor_divide(q_pos, CHUNK)[..., :, None] - jnp.floor_divide(k_pos, CHUNK)[..., None, :]
    visible = (dchunk >= 0) & (dchunk <= WIN_CHUNKS) & (k_pos >= 0)[..., None, :]
    slopes = (2.0 ** (-8.0 * jnp.arange(1, Q_HEADS + 1, dtype=jnp.float32) / Q_HEADS)).reshape(KV_HEADS, GROUP)
    s = s - slopes[:, :, None, None] * jnp.abs(dist)[..., None, None, :, :]
    s = jnp.where(visible[..., None, None, :, :], s, NEG_INF)
    sink = sinks.astype(jnp.float32).reshape(KV_HEADS, GROUP)[:, :, None, None]
    m = jnp.maximum(jnp.max(s, axis=-1, keepdims=True), sink)
    e = jnp.exp(s - m)
    p = e / (jnp.sum(e, axis=-1, keepdims=True) + jnp.exp(sink - m))
    return jnp.einsum('...hgqk,...khd->...qhgd', p.astype(v.dtype), v)


def attn_prompt(q, k, v, sinks):
    bsz, t_len, _ = q.shape
    n_c = t_len // CHUNK
    pad = WIN_CHUNKS * CHUNK
    q = q.reshape(bsz, n_c, CHUNK, KV_HEADS, GROUP, HEAD_DIM)
    k = k.reshape(bsz, t_len, KV_HEADS, HEAD_DIM)
    v = v.reshape(bsz, t_len, KV_HEADS, HEAD_DIM)

    def band(u):
        up = jnp.pad(u, ((0, 0), (pad, 0), (0, 0), (0, 0))).reshape(bsz, n_c + WIN_CHUNKS, CHUNK, KV_HEADS, HEAD_DIM)
        return jnp.concatenate([up[:, i:i + n_c] for i in range(WIN_CHUNKS + 1)], axis=2)

    kp = jnp.arange(-pad, t_len).reshape(n_c + WIN_CHUNKS, CHUNK)
    k_pos = jnp.concatenate([kp[i:i + n_c] for i in range(WIN_CHUNKS + 1)], axis=1)
    q_pos = jnp.arange(t_len).reshape(n_c, CHUNK)
    o = sink_attention(q, band(k), band(v), q_pos, k_pos, sinks)
    return o.reshape(bsz, t_len, D_B), k[:, -CACHE_WIN:], v[:, -CACHE_WIN:]


def attn_sample(q, k, v, sinks, cache_k, cache_v):
    bsz, t_len, _ = q.shape
    q = q.reshape(bsz, t_len, KV_HEADS, GROUP, HEAD_DIM)
    k_all = jnp.concatenate([cache_k.astype(k.dtype), k.reshape(bsz, t_len, KV_HEADS, HEAD_DIM)], axis=1)
    v_all = jnp.concatenate([cache_v.astype(v.dtype), v.reshape(bsz, t_len, KV_HEADS, HEAD_DIM)], axis=1)
    q_pos = PAST_LEN + jnp.arange(t_len)
    k_pos = jnp.concatenate([PAST_LEN - CACHE_WIN + jnp.arange(CACHE_WIN), q_pos])
    o = sink_attention(q, k_all, v_all, q_pos, k_pos, sinks)
    return o.reshape(bsz, t_len, D_B), k_all[:, -CACHE_WIN:], v_all[:, -CACHE_WIN:]


def mixer_layer(x, shift_prev, wkv0, attend, g_norm, w_in, mu, w0, w_w_up, a0, w_a_up, k_k, k_a, r_k,
                lnx_w, lnx_b, p_a, p_b, w_o):
    p_shift, gate_a, q, kb, vb, gate_b, m_a, m_b = project(x, g_norm, w_in)
    y_a, shift_last, wkv_new = rwkv7_branch(p_shift, gate_a, shift_prev, wkv0, mu, w0, w_w_up, a0, w_a_up,
                                            k_k, k_a, r_k, lnx_w, lnx_b)
    o_b, k_rows, v_rows = attend(q, kb, vb)
    y_b = o_b * jax.nn.silu(gate_b)
    merged = jax.nn.sigmoid(m_a) * (y_a @ p_a) + jax.nn.sigmoid(m_b) * (y_b @ p_b)
    return x + merged @ w_o, shift_last, wkv_new, k_rows, v_rows


def setup_inputs(seed: int = 0) -> dict:
    key = jax.random.key(seed)
    ks = jax.random.split(key, 24)
    n = jax.random.normal
    f32 = jnp.float32
    return {
        'x_prompt': n(ks[0], (BATCH, SEQ, D_MODEL), f32),
        'x_sample': n(ks[1], (DEC_BATCH, DEC_SEQ, D_MODEL), f32),
        'state_wkv': 0.5 * n(ks[2], (DEPTH, DEC_BATCH, A_HEADS, HEAD_DIM, HEAD_DIM), f32),
        'state_shift': n(ks[3], (DEPTH, DEC_BATCH, SHIFT_W), f32),
        'cache_k': n(ks[4], (DEPTH, DEC_BATCH, CACHE_WIN, KV_HEADS, HEAD_DIM), f32),
        'cache_v': n(ks[5], (DEPTH, DEC_BATCH, CACHE_WIN, KV_HEADS, HEAD_DIM), f32),
        'g_norm': 1.0 + 0.02 * n(ks[6], (DEPTH, D_MODEL), f32),
        'w_in': n(ks[7], (DEPTH, D_MODEL, IN_W), f32) * D_MODEL ** -0.5,
        'mu_shift': jax.random.uniform(ks[8], (DEPTH, SHIFT_W), f32, 0.2, 0.8),
        'w0': -0.5 + 0.5 * n(ks[9], (DEPTH, D_A), f32),
        'w_w_up': 0.1 * n(ks[10], (DEPTH, R_W, D_A), f32),
        'a0': 0.1 * n(ks[11], (DEPTH, D_A), f32),
        'w_a_up': 0.1 * n(ks[12], (DEPTH, R_A, D_A), f32),
        'k_k': 0.85 + 0.02 * n(ks[13], (DEPTH, D_A), f32),
        'k_a': 1.0 + 0.02 * n(ks[14], (DEPTH, D_A), f32),
        'r_k': 0.1 * n(ks[15], (DEPTH, A_HEADS, HEAD_DIM), f32),
        'lnx_w': 1.0 + 0.02 * n(ks[16], (DEPTH, D_A), f32),
        'lnx_b': 0.02 * n(ks[17], (DEPTH, D_A), f32),
        'sinks': n(ks[18], (DEPTH, Q_HEADS), f32),
        'p_a': n(ks[19], (DEPTH, D_A, D_MODEL), f32) * D_A ** -0.5,
        'p_b': n(ks[20], (DEPTH, D_B, D_MODEL), f32) * D_B ** -0.5,
        'w_o': n(ks[21], (DEPTH, D_MODEL, D_MODEL), f32) * D_MODEL ** -0.5,
        'g_final': 1.0 + 0.02 * n(ks[22], (D_MODEL,), f32),
    }


def reference(x_prompt, x_sample, state_wkv, state_shift, cache_k, cache_v, g_norm, w_in, mu_shift, w0,
              w_w_up, a0, w_a_up, k_k, k_a, r_k, lnx_w, lnx_b, sinks, p_a, p_b, w_o, g_final):
    xp, xs = x_prompt, x_sample
    zero_shift = jnp.zeros((xp.shape[0], SHIFT_W), xp.dtype)
    zero_wkv = jnp.zeros((xp.shape[0], A_HEADS, HEAD_DIM, HEAD_DIM), jnp.float32)
    wkv_p, shift_p, k_p, v_p = [], [], [], []
    wkv_s, shift_s, k_s, v_s = [], [], [], []
    for l in range(DEPTH):
        lw = (g_norm[l], w_in[l], mu_shift[l], w0[l], w_w_up[l], a0[l], w_a_up[l], k_k[l], k_a[l], r_k[l],
              lnx_w[l], lnx_b[l], p_a[l], p_b[l], w_o[l])
        xp, sh, wk, kr, vr = mixer_layer(xp, zero_shift, zero_wkv,
                                         functools.partial(attn_prompt, sinks=sinks[l]), *lw)
        shift_p.append(sh); wkv_p.append(wk); k_p.append(kr); v_p.append(vr)
        xs, sh, wk, kr, vr = mixer_layer(xs, state_shift[l], state_wkv[l],
                                         functools.partial(attn_sample, sinks=sinks[l], cache_k=cache_k[l],
                                                           cache_v=cache_v[l]), *lw)
        shift_s.append(sh); wkv_s.append(wk); k_s.append(kr); v_s.append(vr)
    y_prompt = rms_norm(xp, g_final)
    y_sample = rms_norm(xs, g_final)
    return (y_prompt, y_sample,
            jnp.stack(wkv_p), jnp.stack(shift_p), jnp.stack(k_p), jnp.stack(v_p),
            jnp.stack(wkv_s), jnp.stack(shift_s), jnp.stack(k_s), jnp.stack(v_s))
```

```python
import functools

import jax
import jax.numpy as jnp
from jax import lax
from jax.experimental import pallas as pl
from jax.experimental.pallas import tpu as pltpu

F32 = jnp.float32
BF16 = jnp.bfloat16

D_MODEL = 2048
HEAD_DIM = 64
CHUNK = 64
D_A = 1024
A_HEADS = 16
LORA = 64
SHIFT_W = 3 * D_A + 2 * LORA
D_B = 1024
Q_HEADS = 16
KV_HEADS = 4
GROUP = 4
KV_W = KV_HEADS * HEAD_DIM
WINDOW_CHUNKS = 2
RMS_EPS = 1e-6
LNX_EPS = 64e-5
NEG_INF = -1e30

LANES = 128
PAIRS = D_A // LANES
REST_W = D_A + D_B + 2 * D_MODEL + D_B + 2 * KV_W
VMEM_LIMIT = 56 * 1024 * 1024


def _nn(a, b):
    return jnp.dot(a.astype(BF16), b.astype(BF16), preferred_element_type=F32)


def _nt(a, b):
    return lax.dot_general(a.astype(BF16), b.astype(BF16), (((1,), (1,)), ((), ())),
                           preferred_element_type=F32)


def _sigmoid(x):
    return 1.0 / (1.0 + jnp.exp(-x))


def _proj_kernel(x_ref, g_ref, w_ref, o_ref, h_scr):
    @pl.when(pl.program_id(1) == 0)
    def _():
        x = x_ref[...]
        h = x * lax.rsqrt(jnp.mean(x * x, axis=-1, keepdims=True) + RMS_EPS) * g_ref[...]
        h_scr[...] = h.astype(BF16)

    o_ref[...] = jnp.dot(h_scr[...], w_ref[...], preferred_element_type=F32)


def _proj(x2d, g_row, w, tm, tn):
    m, n = x2d.shape[0], w.shape[1]
    return pl.pallas_call(
        _proj_kernel,
        grid=(m // tm, n // tn),
        in_specs=[pl.BlockSpec((tm, D_MODEL), lambda i, j: (i, 0)),
                  pl.BlockSpec((1, D_MODEL), lambda i, j: (0, 0)),
                  pl.BlockSpec((D_MODEL, tn), lambda i, j: (0, j))],
        out_specs=pl.BlockSpec((tm, tn), lambda i, j: (i, j)),
        out_shape=jax.ShapeDtypeStruct((m, n), F32),
        scratch_shapes=[pltpu.VMEM((tm, D_MODEL), BF16)],
        compiler_params=pltpu.CompilerParams(dimension_semantics=("arbitrary", "arbitrary"),
                                             vmem_limit_bytes=VMEM_LIMIT),
        name="proj",
    )(x2d, g_row, w)


_W0, _A0, _KK, _KA, _RK, _LNW, _LNB, _MU_R, _MU_K, _MU_V, _MU_WA = 0, 1, 2, 3, 4, 5, 6, 8, 9, 10, 11
_CP_ROWS = 16


def _rwkv_kernel(pr_ref, pk_ref, pv_ref, pwa_ref, ga_ref, sr_ref, sk_ref, sv_ref, swa_ref, s0_ref,
                 cp_ref, lw_ref, ya_ref, sout_ref, s_scr, carry_scr, *, n_sub):
    c = pl.program_id(2)
    zero64 = jnp.zeros((HEAD_DIM, HEAD_DIM), F32)

    @pl.when(c == 0)
    def _():
        s_scr[...] = jnp.concatenate(
            [jnp.concatenate([s0_ref[0, 0], zero64], axis=1),
             jnp.concatenate([zero64, s0_ref[0, 1]], axis=1)], axis=0)
        carry_scr[0:1, :] = sr_ref[0]
        carry_scr[1:2, :] = sk_ref[0]
        carry_scr[2:3, :] = sv_ref[0]
        carry_scr[3:4, :] = swa_ref[0]

    cp = cp_ref[...]
    row = lambda r: cp[r:r + 1, :]

    t_idx = lax.broadcasted_iota(jnp.int32, (CHUNK, LANES), 0)
    lane = lax.broadcasted_iota(jnp.int32, (CHUNK, LANES), 1)
    s_idx = lane & (HEAD_DIM - 1)
    low = lane < HEAD_DIM
    strict = s_idx < t_idx
    incl = s_idx <= t_idx
    eye = jnp.where(s_idx == t_idx, 1.0, 0.0).astype(F32)
    r2 = lax.broadcasted_iota(jnp.int32, (LANES, LANES), 0)
    c2 = lax.broadcasted_iota(jnp.int32, (LANES, LANES), 1)
    same_head = (r2 < HEAD_DIM) == (c2 < HEAD_DIM)
    seg_ones = jnp.where(same_head, 1.0, 0.0).astype(BF16)
    rt = lax.broadcasted_iota(jnp.int32, (CHUNK, CHUNK), 0)
    ct = lax.broadcasted_iota(jnp.int32, (CHUNK, CHUNK), 1)
    tril_ones = jnp.where(ct <= rt, 1.0, 0.0).astype(BF16)

    def stack(q):
        return jnp.concatenate([jnp.where(low, q, 0.0), jnp.where(low, 0.0, q)], axis=0)

    def seg_sum(x):
        return jnp.dot(x.astype(BF16), seg_ones, preferred_element_type=F32)

    for ci in range(n_sub):
        rows = slice(ci * CHUNK, (ci + 1) * CHUNK)

        def shifted(ref, carry_row, mu):
            p = ref[0, rows, :]
            prev = jnp.where(t_idx == 0, carry_scr[carry_row:carry_row + 1, :], pltpu.roll(p, 1, 0))
            carry_scr[carry_row:carry_row + 1, :] = p[CHUNK - 1:CHUNK, :]
            return p + mu * (prev - p)

        r = shifted(pr_ref, 0, row(_MU_R))
        k = shifted(pk_ref, 1, row(_MU_K))
        v = shifted(pv_ref, 2, row(_MU_V))
        xwa = shifted(pwa_ref, 3, row(_MU_WA))

        lora = _nn(jnp.where(low, jnp.tanh(xwa), xwa), lw_ref[0])
        wpre = row(_W0) + lora[:, :LANES]
        neg = -wpre
        softplus = jnp.maximum(neg, 0.0) + jnp.log(1.0 + jnp.exp(-jnp.abs(neg)))
        log_decay = -jnp.exp(-softplus - 0.5)
        a_lr = _sigmoid(row(_A0) + lora[:, LANES:])

        kk = k * row(_KK)
        kk = kk / jnp.maximum(jnp.sqrt(seg_sum(kk * kk)), 1e-12)
        k = k * (1.0 + (a_lr - 1.0) * row(_KA))
        a_vec = -kk
        b_vec = kk * a_lr

        ld_hi = log_decay.astype(BF16)
        res = log_decay - ld_hi.astype(F32)
        ld_mid = res.astype(BF16)
        ld_lo = (res - ld_mid.astype(F32)).astype(BF16)
        cum = (jnp.dot(tril_ones, ld_hi, preferred_element_type=F32)
               + jnp.dot(tril_ones, ld_mid, preferred_element_type=F32)
               + jnp.dot(tril_ones, ld_lo, preferred_element_type=F32))
        cum_last = cum[CHUNK - 1:CHUNK, :]
        g_inv = jnp.exp(-cum)
        g_tail = jnp.exp(cum_last - cum)
        a_t = a_vec * jnp.exp(cum - log_decay)
        r_t = r * jnp.exp(cum)
        stk_b = stack(b_vec * g_inv)
        stk_k = stack(k * g_inv)
        stk_v = stack(v)

        a_ab = jnp.where(strict, _nt(a_t, stk_b), 0.0)
        a_ak = jnp.where(strict, _nt(a_t, stk_k), 0.0)
        a_rb = jnp.where(incl, _nt(r_t, stk_b), 0.0)
        a_rk = jnp.where(incl, _nt(r_t, stk_k), 0.0)

        power = a_ab
        inv = eye + a_ab
        for _ in range(5):
            power = _nn(power, stack(power))
            inv = inv + _nn(inv, stack(power))

        s_prev = s_scr[...]
        x = _nt(a_t, s_prev) + _nn(a_ak, stk_v)
        u = _nn(inv, stack(x))
        y = _nt(r_t, s_prev) + _nn(a_rb, stack(u)) + _nn(a_rk, stk_v)
        uv = jnp.concatenate([u, v], axis=0)
        bk = jnp.concatenate([b_vec * g_tail, k * g_tail], axis=0)
        s_scr[...] = s_prev * jnp.exp(cum_last) + jnp.where(same_head, _nn(uv.T, bk), 0.0)

        mean = seg_sum(y) * (1.0 / HEAD_DIM)
        d = y - mean
        var = seg_sum(d * d) * (1.0 / HEAD_DIM)
        yn = d * lax.rsqrt(var + LNX_EPS) * row(_LNW) + row(_LNB)
        yn = yn + seg_sum(r * k * row(_RK)) * v
        gate = ga_ref[0, rows, :]
        ya_ref[0, rows, :] = (yn * (gate * _sigmoid(gate))).astype(BF16)

    @pl.when(c == pl.num_programs(2) - 1)
    def _():
        s = s_scr[...]
        sout_ref[0, 0] = s[:HEAD_DIM, :HEAD_DIM]
        sout_ref[0, 1] = s[HEAD_DIM:, HEAD_DIM:]


def _rwkv(p3, zr3, shift0, wkv0, cparams, lora_w, n_sub):
    bsz, t_len, _ = p3.shape
    tb = n_sub * CHUNK
    tok = lambda off: pl.BlockSpec((1, tb, LANES), lambda b, p, c: (b, c, off + p))
    sh = lambda off: pl.BlockSpec((1, 1, LANES), lambda b, p, c: (b, 0, off + p))
    wa_col = 3 * PAIRS
    return pl.pallas_call(
        functools.partial(_rwkv_kernel, n_sub=n_sub),
        grid=(bsz, PAIRS, t_len // tb),
        in_specs=[tok(0), tok(PAIRS), tok(2 * PAIRS),
                  pl.BlockSpec((1, tb, LANES), lambda b, p, c: (b, c, wa_col)),
                  tok(0),
                  sh(0), sh(PAIRS), sh(2 * PAIRS),
                  pl.BlockSpec((1, 1, LANES), lambda b, p, c: (b, 0, wa_col)),
                  pl.BlockSpec((1, 2, HEAD_DIM, HEAD_DIM), lambda b, p, c: (b, p, 0, 0)),
                  pl.BlockSpec((_CP_ROWS, LANES), lambda b, p, c: (0, p)),
                  pl.BlockSpec((1, LANES, 2 * LANES), lambda b, p, c: (p, 0, 0))],
        out_specs=[pl.BlockSpec((1, tb, LANES), lambda b, p, c: (b, c, p)),
                   pl.BlockSpec((1, 2, HEAD_DIM, HEAD_DIM), lambda b, p, c: (b, p, 0, 0))],
        out_shape=[jax.ShapeDtypeStruct((bsz, t_len, D_A), BF16),
                   jax.ShapeDtypeStruct((bsz, A_HEADS, HEAD_DIM, HEAD_DIM), F32)],
        scratch_shapes=[pltpu.VMEM((LANES, LANES), F32), pltpu.VMEM((8, LANES), F32)],
        compiler_params=pltpu.CompilerParams(dimension_semantics=("arbitrary",) * 3),
        name="rwkv",
    )(p3, p3, p3, p3, zr3, shift0, shift0, shift0, shift0, wkv0, cparams, lora_w)


_SLOPES = tuple(2.0 ** (-8.0 * (h + 1) / Q_HEADS) for h in range(Q_HEADS))


def _attn_kernel(sinks_ref, q_ref, k0_ref, k1_ref, k2_ref, v0_ref, v1_ref, v2_ref, gb_ref, yb_ref, *,
                 masked_history):
    c = pl.program_id(1)
    q = q_ref[0]
    k_all = jnp.concatenate([k0_ref[0], k1_ref[0], k2_ref[0]], axis=0)
    v_all = jnp.concatenate([v0_ref[0], v1_ref[0], v2_ref[0]], axis=0)
    n_q, n_k = GROUP * CHUNK, (WINDOW_CHUNKS + 1) * CHUNK
    ri = lax.broadcasted_iota(jnp.int32, (n_q, n_k), 0)
    kj = lax.broadcasted_iota(jnp.int32, (n_q, n_k), 1)
    dist = jnp.abs((ri & (CHUNK - 1)) + WINDOW_CHUNKS * CHUNK - kj).astype(F32)
    g_idx = lax.shift_right_logical(lax.broadcasted_iota(jnp.int32, (n_q, 1), 0), CHUNK.bit_length() - 1)
    if masked_history:
        visible = (kj >= 2 * CHUNK) | ((kj >= CHUNK) & (c >= 1)) | (c >= 2)

    def per_group(vals):
        out = jnp.full((n_q, 1), vals[GROUP - 1], F32)
        for g in range(GROUP - 2, -1, -1):
            out = jnp.where(g_idx == g, vals[g], out)
        return out

    outs = []
    for h in range(KV_HEADS):
        kh = k_all[:, h * HEAD_DIM:(h + 1) * HEAD_DIM]
        vh = v_all[:, h * HEAD_DIM:(h + 1) * HEAD_DIM]
        qh = jnp.concatenate([q[:, (h * GROUP + g) * HEAD_DIM:(h * GROUP + g + 1) * HEAD_DIM]
                              for g in range(GROUP)], axis=0)
        s = _nt(qh, kh) * (HEAD_DIM ** -0.5)
        s = s - per_group([_SLOPES[h * GROUP + g] for g in range(GROUP)]) * dist
        if masked_history:
            s = jnp.where(visible, s, NEG_INF)
        sink = per_group([sinks_ref[h * GROUP + g] for g in range(GROUP)])
        m = jnp.maximum(jnp.max(s, axis=-1, keepdims=True), sink)
        e = jnp.exp(s - m)
        den = jnp.sum(e, axis=-1, keepdims=True) + jnp.exp(sink - m)
        o = _nn(e, vh) / den
        outs += [o[g * CHUNK:(g + 1) * CHUNK] for g in range(GROUP)]
    o_all = jnp.concatenate(outs, axis=1)
    gate = gb_ref[0]
    yb_ref[0] = (o_all * (gate * _sigmoid(gate))).astype(BF16)


def _attn(zr3, sinks, hist_k, hist_v):
    bsz, t_len, _ = zr3.shape
    q_col = (2 * D_A + 2 * D_MODEL) // D_B
    k_col = (2 * D_A + 2 * D_MODEL + D_B) // KV_W
    v_col = k_col + 1
    masked_history = hist_k is None
    if masked_history:
        prev = lambda col, back: pl.BlockSpec((1, CHUNK, KV_W),
                                              lambda b, c: (b, jnp.maximum(c - back, 0), col))
        k_specs = [prev(k_col, 2), prev(k_col, 1)]
        v_specs = [prev(v_col, 2), prev(v_col, 1)]
        k_args, v_args = [zr3, zr3], [zr3, zr3]
    else:
        assert t_len == CHUNK
        hist = lambda i: pl.BlockSpec((1, CHUNK, KV_W), lambda b, c: (b, i, 0))
        k_specs = v_specs = [hist(0), hist(1)]
        k_args, v_args = [hist_k, hist_k], [hist_v, hist_v]
    cur = lambda col: pl.BlockSpec((1, CHUNK, KV_W), lambda b, c: (b, c, col))
    return pl.pallas_call(
        functools.partial(_attn_kernel, masked_history=masked_history),
        grid=(bsz, t_len // CHUNK),
        in_specs=[pl.BlockSpec(memory_space=pltpu.SMEM),
                  pl.BlockSpec((1, CHUNK, D_B), lambda b, c: (b, c, q_col))]
                 + k_specs + [cur(k_col)] + v_specs + [cur(v_col)]
                 + [pl.BlockSpec((1, CHUNK, D_B), lambda b, c: (b, c, 1))],
        out_specs=pl.BlockSpec((1, CHUNK, D_B), lambda b, c: (b, c, 0)),
        out_shape=jax.ShapeDtypeStruct((bsz, t_len, D_B), BF16),
        compiler_params=pltpu.CompilerParams(dimension_semantics=("arbitrary", "arbitrary")),
        name="attn",
    )(sinks, zr3, *k_args, zr3, *v_args, zr3, zr3)


def _merge_kernel(ya_ref, yb_ref, ma_ref, mb_ref, x_ref, pa_ref, pb_ref, wo_ref, gf_ref, o_ref):
    ua = jnp.dot(ya_ref[...], pa_ref[...], preferred_element_type=F32)
    ub = jnp.dot(yb_ref[...], pb_ref[...], preferred_element_type=F32)
    merged = _sigmoid(ma_ref[...]) * ua + _sigmoid(mb_ref[...]) * ub
    o = x_ref[...] + jnp.dot(merged.astype(BF16), wo_ref[...], preferred_element_type=F32)
    o_ref[...] = o * lax.rsqrt(jnp.mean(o * o, axis=-1, keepdims=True) + RMS_EPS) * gf_ref[...]


def _merge(ya, yb, zr, x2d, pa, pb, wo, gf_row, tm):
    m = x2d.shape[0]
    whole = lambda shape: pl.BlockSpec(shape, lambda i: (0, 0), pipeline_mode=pl.Buffered(1))
    return pl.pallas_call(
        _merge_kernel,
        grid=(m // tm,),
        in_specs=[pl.BlockSpec((tm, D_A), lambda i: (i, 0)),
                  pl.BlockSpec((tm, D_B), lambda i: (i, 0)),
                  pl.BlockSpec((tm, D_MODEL), lambda i: (i, 1)),
                  pl.BlockSpec((tm, D_MODEL), lambda i: (i, 2)),
                  pl.BlockSpec((tm, D_MODEL), lambda i: (i, 0)),
                  whole((D_A, D_MODEL)), whole((D_B, D_MODEL)), whole((D_MODEL, D_MODEL)),
                  whole((1, D_MODEL))],
        out_specs=pl.BlockSpec((tm, D_MODEL), lambda i: (i, 0)),
        out_shape=jax.ShapeDtypeStruct((m, D_MODEL), F32),
        compiler_params=pltpu.CompilerParams(dimension_semantics=("arbitrary",),
                                             vmem_limit_bytes=VMEM_LIMIT),
        name="merge",
    )(ya, yb, zr, zr, x2d, pa, pb, wo, gf_row)


def _layer(x, shift0, wkv0, hist_k, hist_v, wts):
    bsz, t_len, _ = x.shape
    m = bsz * t_len
    x2d = x.reshape(m, D_MODEL)
    p = _proj(x2d, wts["g_norm"], wts["w_shift"], tm=1024, tn=640)
    zr = _proj(x2d, wts["g_norm"], wts["w_rest"], tm=1024, tn=768)
    p3 = p.reshape(bsz, t_len, SHIFT_W)
    zr3 = zr.reshape(bsz, t_len, REST_W)
    ya, wkv = _rwkv(p3, zr3, shift0.reshape(bsz, 1, SHIFT_W), wkv0, wts["cparams"], wts["lora_w"], n_sub=1)
    yb = _attn(zr3, wts["sinks"], hist_k, hist_v)
    y = _merge(ya.reshape(m, D_A), yb.reshape(m, D_B), zr, x2d, wts["p_a"], wts["p_b"], wts["w_o"],
               wts["g_final"], tm=256)
    k_off = 2 * D_A + 2 * D_MODEL + D_B
    k_new = zr3[:, :, k_off:k_off + KV_W]
    v_new = zr3[:, :, k_off + KV_W:k_off + 2 * KV_W]
    return y.reshape(bsz, t_len, D_MODEL), wkv, p3[:, -1, :], k_new, v_new


def _prepare_weights(g_norm, w_in, mu_shift, w0, w_w_up, a0, w_a_up, k_k, k_a, r_k, lnx_w, lnx_b, sinks,
                     p_a, p_b, w_o, g_final):
    o_ga = SHIFT_W
    o_q = o_ga + D_A
    o_k = o_q + D_B
    o_gb = o_k + 2 * KV_W
    o_ma = o_gb + D_B
    w_shift = w_in[:, :SHIFT_W].astype(BF16)
    w_rest = jnp.concatenate([w_in[:, o_ga:o_q], w_in[:, o_gb:o_ma], w_in[:, o_ma:], w_in[:, o_q:o_gb]],
                             axis=1).astype(BF16)
    zeros = jnp.zeros((D_A,), F32)
    cparams = jnp.stack([w0, a0, k_k, k_a, r_k.reshape(D_A), lnx_w, lnx_b, zeros,
                         mu_shift[:D_A], mu_shift[D_A:2 * D_A], mu_shift[2 * D_A:3 * D_A],
                         jnp.tile(mu_shift[3 * D_A:], PAIRS), zeros, zeros, zeros, zeros])
    ww = w_w_up.reshape(LORA, PAIRS, LANES).transpose(1, 0, 2)
    wa = w_a_up.reshape(LORA, PAIRS, LANES).transpose(1, 0, 2)
    z = jnp.zeros_like(ww)
    lora_w = jnp.concatenate([jnp.concatenate([ww, z], axis=2), jnp.concatenate([z, wa], axis=2)],
                             axis=1).astype(BF16)
    return dict(g_norm=g_norm.reshape(1, D_MODEL), w_shift=w_shift, w_rest=w_rest, cparams=cparams,
                lora_w=lora_w, sinks=sinks, p_a=p_a.astype(BF16), p_b=p_b.astype(BF16),
                w_o=w_o.astype(BF16), g_final=g_final.reshape(1, D_MODEL))


def kernel(x_prompt, x_sample, state_wkv, state_shift, cache_k, cache_v, g_norm, w_in, mu_shift, w0, w_w_up,
           a0, w_a_up, k_k, k_a, r_k, lnx_w, lnx_b, sinks, p_a, p_b, w_o, g_final):
    assert g_norm.shape[0] == 1, "single-layer stack"
    wts = _prepare_weights(g_norm[0], w_in[0], mu_shift[0], w0[0], w_w_up[0], a0[0], w_a_up[0], k_k[0],
                           k_a[0], r_k[0], lnx_w[0], lnx_b[0], sinks[0], p_a[0], p_b[0], w_o[0], g_final)
    n_p, n_s = x_prompt.shape[0], x_sample.shape[0]
    cache_win = cache_k.shape[2]
    assert cache_win == WINDOW_CHUNKS * CHUNK

    y_p, wkv_p, shift_p, k_p, v_p = _layer(
        x_prompt, jnp.zeros((n_p, SHIFT_W), F32), jnp.zeros((n_p, A_HEADS, HEAD_DIM, HEAD_DIM), F32),
        None, None, wts)
    hist_k = cache_k[0].reshape(n_s, cache_win, KV_W)
    hist_v = cache_v[0].reshape(n_s, cache_win, KV_W)
    y_s, wkv_s, shift_s, k_s, v_s = _layer(x_sample, state_shift[0], state_wkv[0], hist_k, hist_v, wts)

    rows = lambda u, n: u[:, -cache_win:].reshape(n, cache_win, KV_HEADS, HEAD_DIM)[None]
    k_s = jnp.concatenate([hist_k, k_s], axis=1)
    v_s = jnp.concatenate([hist_v, v_s], axis=1)
    return (y_p, y_s,
            wkv_p[None], shift_p[None], rows(k_p, n_p), rows(v_p, n_p),
            wkv_s[None], shift_s[None], rows(k_s, n_s), rows(v_s, n_s))
```

```python
import functools

import jax
import jax.numpy as jnp
from jax import lax
from jax.experimental import pallas as pl
from jax.experimental.pallas import tpu as pltpu

F32 = jnp.float32
BF16 = jnp.bfloat16

D_MODEL = 2048
HEAD_DIM = 64
CHUNK = 64
D_A = 1024
A_HEADS = 16
LORA = 64
SHIFT_W = 3 * D_A + 2 * LORA
D_B = 1024
Q_HEADS = 16
KV_HEADS = 4
GROUP = 4
KV_W = KV_HEADS * HEAD_DIM
WINDOW_CHUNKS = 2
RMS_EPS = 1e-6
LNX_EPS = 64e-5
NEG_INF = -1e30

LANES = 128
PAIRS = D_A // LANES
REST_W = D_A + D_B + 2 * D_MODEL + D_B + 2 * KV_W
VMEM_LIMIT = 56 * 1024 * 1024


def _nn(a, b):
    return jnp.dot(a.astype(BF16), b.astype(BF16), preferred_element_type=F32)


def _nt(a, b):
    return lax.dot_general(a.astype(BF16), b.astype(BF16), (((1,), (1,)), ((), ())),
                           preferred_element_type=F32)


def _sigmoid(x):
    return 1.0 / (1.0 + jnp.exp(-x))


def _proj_kernel(x_ref, g_ref, w_ref, o_ref, h_scr):
    @pl.when(pl.program_id(1) == 0)
    def _():
        x = x_ref[...]
        h = x * lax.rsqrt(jnp.mean(x * x, axis=-1, keepdims=True) + RMS_EPS) * g_ref[...]
        h_scr[...] = h.astype(BF16)

    o_ref[...] = jnp.dot(h_scr[...], w_ref[...], preferred_element_type=F32)


def _proj(x2d, g_row, w, tm, tn):
    m, n = x2d.shape[0], w.shape[1]
    return pl.pallas_call(
        _proj_kernel,
        grid=(m // tm, n // tn),
        in_specs=[pl.BlockSpec((tm, D_MODEL), lambda i, j: (i, 0)),
                  pl.BlockSpec((1, D_MODEL), lambda i, j: (0, 0)),
                  pl.BlockSpec((D_MODEL, tn), lambda i, j: (0, j))],
        out_specs=pl.BlockSpec((tm, tn), lambda i, j: (i, j)),
        out_shape=jax.ShapeDtypeStruct((m, n), F32),
        scratch_shapes=[pltpu.VMEM((tm, D_MODEL), BF16)],
        compiler_params=pltpu.CompilerParams(dimension_semantics=("arbitrary", "arbitrary"),
                                             vmem_limit_bytes=VMEM_LIMIT),
        name="proj",
    )(x2d, g_row, w)


_W0, _A0, _KK, _KA, _RK, _LNW, _LNB, _MU_R, _MU_K, _MU_V, _MU_WA = 0, 1, 2, 3, 4, 5, 6, 8, 9, 10, 11
_CP_ROWS = 16


def _rwkv_kernel(p_ref, ga_ref, sh_ref, s0_ref, cp_ref, lw_ref, ya_ref, sout_ref, s_scr, carry_scr, *, n_sub):
    c = pl.program_id(1)
    zero64 = jnp.zeros((HEAD_DIM, HEAD_DIM), F32)

    @pl.when(c == 0)
    def _():
        for pair in range(PAIRS):
            s_scr[pair] = jnp.concatenate(
                [jnp.concatenate([s0_ref[0, 2 * pair], zero64], axis=1),
                 jnp.concatenate([zero64, s0_ref[0, 2 * pair + 1]], axis=1)], axis=0)
        carry_scr[0:1, :] = sh_ref[0]

    t_idx = lax.broadcasted_iota(jnp.int32, (CHUNK, LANES), 0)
    lane = lax.broadcasted_iota(jnp.int32, (CHUNK, LANES), 1)
    s_idx = lane & (HEAD_DIM - 1)
    low = lane < HEAD_DIM
    strict = s_idx < t_idx
    incl = s_idx <= t_idx
    eye = jnp.where(s_idx == t_idx, 1.0, 0.0).astype(F32)
    r2 = lax.broadcasted_iota(jnp.int32, (LANES, LANES), 0)
    c2 = lax.broadcasted_iota(jnp.int32, (LANES, LANES), 1)
    same_head = (r2 < HEAD_DIM) == (c2 < HEAD_DIM)
    seg_ones = jnp.where(same_head, 1.0, 0.0).astype(BF16)
    rt = lax.broadcasted_iota(jnp.int32, (CHUNK, CHUNK), 0)
    ct = lax.broadcasted_iota(jnp.int32, (CHUNK, CHUNK), 1)
    tril_ones = jnp.where(ct <= rt, 1.0, 0.0).astype(BF16)

    def stack(q):
        return jnp.concatenate([jnp.where(low, q, 0.0), jnp.where(low, 0.0, q)], axis=0)

    def seg_sum(x):
        return jnp.dot(x.astype(BF16), seg_ones, preferred_element_type=F32)

    wa_off = 3 * D_A
    each = range(PAIRS)
    pair_cols = [slice(pair * LANES, (pair + 1) * LANES) for pair in each]

    for ci in range(n_sub):
        rows = slice(ci * CHUNK, (ci + 1) * CHUNK)
        row = lambda r, pair: cp_ref[r:r + 1, pair_cols[pair]]

        def shifted(off, mu):
            lanes = slice(off, off + LANES)
            p = p_ref[0, rows, lanes]
            if ci == 0:
                last = carry_scr[0:1, lanes]
            else:
                last = p_ref[0, ci * CHUNK - 1:ci * CHUNK, lanes]
            prev = jnp.where(t_idx == 0, last, pltpu.roll(p, 1, 0))
            return p + mu * (prev - p)

        r = [shifted(i * LANES, row(_MU_R, i)) for i in each]
        k_raw = [shifted(D_A + i * LANES, row(_MU_K, i)) for i in each]
        v = [shifted(2 * D_A + i * LANES, row(_MU_V, i)) for i in each]
        xwa = shifted(wa_off, row(_MU_WA, 0))
        lora_in = jnp.where(low, jnp.tanh(xwa), xwa)

        lora = [_nn(lora_in, lw_ref[i]) for i in each]
        neg = [-(row(_W0, i) + lora[i][:, :LANES]) for i in each]
        softplus = [jnp.maximum(neg[i], 0.0) + jnp.log(1.0 + jnp.exp(-jnp.abs(neg[i]))) for i in each]
        log_decay = [-jnp.exp(-softplus[i] - 0.5) for i in each]
        a_lr = [_sigmoid(row(_A0, i) + lora[i][:, LANES:]) for i in each]

        kk = [k_raw[i] * row(_KK, i) for i in each]
        kk_norm = [jnp.maximum(jnp.sqrt(seg_sum(kk[i] * kk[i])), 1e-12) for i in each]
        kk = [kk[i] / kk_norm[i] for i in each]
        k = [k_raw[i] * (1.0 + (a_lr[i] - 1.0) * row(_KA, i)) for i in each]
        b_vec = [kk[i] * a_lr[i] for i in each]

        def prefix_sum(ld):
            hi = ld.astype(BF16)
            res = ld - hi.astype(F32)
            mid = res.astype(BF16)
            lo = (res - mid.astype(F32)).astype(BF16)
            return (jnp.dot(tril_ones, hi, preferred_element_type=F32)
                    + jnp.dot(tril_ones, mid, preferred_element_type=F32)
                    + jnp.dot(tril_ones, lo, preferred_element_type=F32))

        cum = [prefix_sum(log_decay[i]) for i in each]
        cum_last = [cum[i][CHUNK - 1:CHUNK, :] for i in each]
        g_inv = [jnp.exp(-cum[i]) for i in each]
        g_tail = [jnp.exp(cum_last[i] - cum[i]) for i in each]
        a_t = [-kk[i] * jnp.exp(cum[i] - log_decay[i]) for i in each]
        r_t = [r[i] * jnp.exp(cum[i]) for i in each]
        stk_b = [stack(b_vec[i] * g_inv[i]) for i in each]
        stk_k = [stack(k[i] * g_inv[i]) for i in each]
        stk_v = [stack(v[i]) for i in each]

        a_ab = [jnp.where(strict, _nt(a_t[i], stk_b[i]), 0.0) for i in each]
        a_ak = [jnp.where(strict, _nt(a_t[i], stk_k[i]), 0.0) for i in each]
        a_rb = [jnp.where(incl, _nt(r_t[i], stk_b[i]), 0.0) for i in each]
        a_rk = [jnp.where(incl, _nt(r_t[i], stk_k[i]), 0.0) for i in each]

        power = a_ab
        inv = [eye + a_ab[i] for i in each]
        for _ in range(5):
            power = [_nn(power[i], stack(power[i])) for i in each]
            inv = [inv[i] + _nn(inv[i], stack(power[i])) for i in each]

        s_prev = [s_scr[i] for i in each]
        x = [_nt(a_t[i], s_prev[i]) + _nn(a_ak[i], stk_v[i]) for i in each]
        u = [_nn(inv[i], stack(x[i])) for i in each]
        y = [_nt(r_t[i], s_prev[i]) + _nn(a_rb[i], stack(u[i])) + _nn(a_rk[i], stk_v[i]) for i in each]
        for i in each:
            uv = jnp.concatenate([u[i], v[i]], axis=0)
            bk = jnp.concatenate([b_vec[i] * g_tail[i], k[i] * g_tail[i]], axis=0)
            s_scr[i] = s_prev[i] * jnp.exp(cum_last[i]) + jnp.where(same_head, _nn(uv.T, bk), 0.0)

        mean = [seg_sum(y[i]) * (1.0 / HEAD_DIM) for i in each]
        d = [y[i] - mean[i] for i in each]
        var = [seg_sum(d[i] * d[i]) * (1.0 / HEAD_DIM) for i in each]
        bonus = [seg_sum(r[i] * k[i] * row(_RK, i)) for i in each]
        for i in each:
            yn = d[i] * lax.rsqrt(var[i] + LNX_EPS) * row(_LNW, i) + row(_LNB, i) + bonus[i] * v[i]
            gate = ga_ref[0, rows, pair_cols[i]]
            ya_ref[0, rows, pair_cols[i]] = (yn * (gate * _sigmoid(gate))).astype(BF16)

    carry_scr[0:1, :] = p_ref[0, n_sub * CHUNK - 1:n_sub * CHUNK, :]

    @pl.when(c == pl.num_programs(1) - 1)
    def _():
        for pair in range(PAIRS):
            s = s_scr[pair]
            sout_ref[0, 2 * pair] = s[:HEAD_DIM, :HEAD_DIM]
            sout_ref[0, 2 * pair + 1] = s[HEAD_DIM:, HEAD_DIM:]


def _rwkv(p3, zr3, shift0, wkv0, cparams, lora_w, n_sub):
    bsz, t_len, _ = p3.shape
    tb = n_sub * CHUNK
    state_spec = pl.BlockSpec((1, A_HEADS, HEAD_DIM, HEAD_DIM), lambda b, c: (b, 0, 0, 0))
    return pl.pallas_call(
        functools.partial(_rwkv_kernel, n_sub=n_sub),
        grid=(bsz, t_len // tb),
        in_specs=[pl.BlockSpec((1, tb, SHIFT_W), lambda b, c: (b, c, 0)),
                  pl.BlockSpec((1, tb, D_A), lambda b, c: (b, c, 0)),
                  pl.BlockSpec((1, 1, SHIFT_W), lambda b, c: (b, 0, 0)),
                  state_spec,
                  pl.BlockSpec((_CP_ROWS, D_A), lambda b, c: (0, 0)),
                  pl.BlockSpec((PAIRS, LANES, 2 * LANES), lambda b, c: (0, 0, 0))],
        out_specs=[pl.BlockSpec((1, tb, D_A), lambda b, c: (b, c, 0)), state_spec],
        out_shape=[jax.ShapeDtypeStruct((bsz, t_len, D_A), BF16),
                   jax.ShapeDtypeStruct((bsz, A_HEADS, HEAD_DIM, HEAD_DIM), F32)],
        scratch_shapes=[pltpu.VMEM((PAIRS, LANES, LANES), F32), pltpu.VMEM((8, SHIFT_W), F32)],
        compiler_params=pltpu.CompilerParams(dimension_semantics=("arbitrary", "arbitrary")),
        name="rwkv",
    )(p3, zr3, shift0, wkv0, cparams, lora_w)


_SLOPES = tuple(2.0 ** (-8.0 * (h + 1) / Q_HEADS) for h in range(Q_HEADS))


def _attn_kernel(sinks_ref, q_ref, k0_ref, k1_ref, k2_ref, v0_ref, v1_ref, v2_ref, gb_ref, yb_ref, *,
                 masked_history):
    c = pl.program_id(1)
    q = q_ref[0]
    k_all = jnp.concatenate([k0_ref[0], k1_ref[0], k2_ref[0]], axis=0)
    v_all = jnp.concatenate([v0_ref[0], v1_ref[0], v2_ref[0]], axis=0)
    n_q, n_k = GROUP * CHUNK, (WINDOW_CHUNKS + 1) * CHUNK
    ri = lax.broadcasted_iota(jnp.int32, (n_q, n_k), 0)
    kj = lax.broadcasted_iota(jnp.int32, (n_q, n_k), 1)
    dist = jnp.abs((ri & (CHUNK - 1)) + WINDOW_CHUNKS * CHUNK - kj).astype(F32)
    g_idx = lax.shift_right_logical(lax.broadcasted_iota(jnp.int32, (n_q, 1), 0), CHUNK.bit_length() - 1)
    if masked_history:
        visible = (kj >= 2 * CHUNK) | ((kj >= CHUNK) & (c >= 1)) | (c >= 2)

    def per_group(vals):
        out = jnp.full((n_q, 1), vals[GROUP - 1], F32)
        for g in range(GROUP - 2, -1, -1):
            out = jnp.where(g_idx == g, vals[g], out)
        return out

    outs = []
    for h in range(KV_HEADS):
        kh = k_all[:, h * HEAD_DIM:(h + 1) * HEAD_DIM]
        vh = v_all[:, h * HEAD_DIM:(h + 1) * HEAD_DIM]
        qh = jnp.concatenate([q[:, (h * GROUP + g) * HEAD_DIM:(h * GROUP + g + 1) * HEAD_DIM]
                              for g in range(GROUP)], axis=0)
        s = _nt(qh, kh) * (HEAD_DIM ** -0.5)
        s = s - per_group([_SLOPES[h * GROUP + g] for g in range(GROUP)]) * dist
        if masked_history:
            s = jnp.where(visible, s, NEG_INF)
        sink = per_group([sinks_ref[h * GROUP + g] for g in range(GROUP)])
        m = jnp.maximum(jnp.max(s, axis=-1, keepdims=True), sink)
        e = jnp.exp(s - m)
        den = jnp.sum(e, axis=-1, keepdims=True) + jnp.exp(sink - m)
        o = _nn(e, vh) / den
        outs += [o[g * CHUNK:(g + 1) * CHUNK] for g in range(GROUP)]
    o_all = jnp.concatenate(outs, axis=1)
    gate = gb_ref[0]
    yb_ref[0] = (o_all * (gate * _sigmoid(gate))).astype(BF16)


def _attn(zr3, sinks, hist_k, hist_v):
    bsz, t_len, _ = zr3.shape
    q_col = (2 * D_A + 2 * D_MODEL) // D_B
    k_col = (2 * D_A + 2 * D_MODEL + D_B) // KV_W
    v_col = k_col + 1
    masked_history = hist_k is None
    if masked_history:
        prev = lambda col, back: pl.BlockSpec((1, CHUNK, KV_W),
                                              lambda b, c: (b, jnp.maximum(c - back, 0), col))
        k_specs = [prev(k_col, 2), prev(k_col, 1)]
        v_specs = [prev(v_col, 2), prev(v_col, 1)]
        k_args, v_args = [zr3, zr3], [zr3, zr3]
    else:
        assert t_len == CHUNK
        hist = lambda i: pl.BlockSpec((1, CHUNK, KV_W), lambda b, c: (b, i, 0))
        k_specs = v_specs = [hist(0), hist(1)]
        k_args, v_args = [hist_k, hist_k], [hist_v, hist_v]
    cur = lambda col: pl.BlockSpec((1, CHUNK, KV_W), lambda b, c: (b, c, col))
    return pl.pallas_call(
        functools.partial(_attn_kernel, masked_history=masked_history),
        grid=(bsz, t_len // CHUNK),
        in_specs=[pl.BlockSpec(memory_space=pltpu.SMEM),
                  pl.BlockSpec((1, CHUNK, D_B), lambda b, c: (b, c, q_col))]
                 + k_specs + [cur(k_col)] + v_specs + [cur(v_col)]
                 + [pl.BlockSpec((1, CHUNK, D_B), lambda b, c: (b, c, 1))],
        out_specs=pl.BlockSpec((1, CHUNK, D_B), lambda b, c: (b, c, 0)),
        out_shape=jax.ShapeDtypeStruct((bsz, t_len, D_B), BF16),
        compiler_params=pltpu.CompilerParams(dimension_semantics=("arbitrary", "arbitrary")),
        name="attn",
    )(sinks, zr3, *k_args, zr3, *v_args, zr3, zr3)


def _merge_kernel(ya_ref, yb_ref, ma_ref, mb_ref, x_ref, pa_ref, pb_ref, wo_ref, gf_ref, o_ref):
    ua = jnp.dot(ya_ref[...], pa_ref[...], preferred_element_type=F32)
    ub = jnp.dot(yb_ref[...], pb_ref[...], preferred_element_type=F32)
    merged = _sigmoid(ma_ref[...]) * ua + _sigmoid(mb_ref[...]) * ub
    o = x_ref[...] + jnp.dot(merged.astype(BF16), wo_ref[...], preferred_element_type=F32)
    o_ref[...] = o * lax.rsqrt(jnp.mean(o * o, axis=-1, keepdims=True) + RMS_EPS) * gf_ref[...]


def _merge(ya, yb, zr, x2d, pa, pb, wo, gf_row, tm):
    m = x2d.shape[0]
    whole = lambda shape: pl.BlockSpec(shape, lambda i: (0, 0), pipeline_mode=pl.Buffered(1))
    return pl.pallas_call(
        _merge_kernel,
        grid=(m // tm,),
        in_specs=[pl.BlockSpec((tm, D_A), lambda i: (i, 0)),
                  pl.BlockSpec((tm, D_B), lambda i: (i, 0)),
                  pl.BlockSpec((tm, D_MODEL), lambda i: (i, 1)),
                  pl.BlockSpec((tm, D_MODEL), lambda i: (i, 2)),
                  pl.BlockSpec((tm, D_MODEL), lambda i: (i, 0)),
                  whole((D_A, D_MODEL)), whole((D_B, D_MODEL)), whole((D_MODEL, D_MODEL)),
                  whole((1, D_MODEL))],
        out_specs=pl.BlockSpec((tm, D_MODEL), lambda i: (i, 0)),
        out_shape=jax.ShapeDtypeStruct((m, D_MODEL), F32),
        compiler_params=pltpu.CompilerParams(dimension_semantics=("arbitrary",),
                                             vmem_limit_bytes=VMEM_LIMIT),
        name="merge",
    )(ya, yb, zr, zr, x2d, pa, pb, wo, gf_row)


def _layer(x, shift0, wkv0, hist_k, hist_v, wts):
    bsz, t_len, _ = x.shape
    m = bsz * t_len
    x2d = x.reshape(m, D_MODEL)
    p = _proj(x2d, wts["g_norm"], wts["w_shift"], tm=1024, tn=640)
    zr = _proj(x2d, wts["g_norm"], wts["w_rest"], tm=1024, tn=768)
    p3 = p.reshape(bsz, t_len, SHIFT_W)
    zr3 = zr.reshape(bsz, t_len, REST_W)
    ya, wkv = _rwkv(p3, zr3, shift0.reshape(bsz, 1, SHIFT_W), wkv0, wts["cparams"], wts["lora_w"], n_sub=1)
    yb = _attn(zr3, wts["sinks"], hist_k, hist_v)
    y = _merge(ya.reshape(m, D_A), yb.reshape(m, D_B), zr, x2d, wts["p_a"], wts["p_b"], wts["w_o"],
               wts["g_final"], tm=256)
    k_off = 2 * D_A + 2 * D_MODEL + D_B
    k_new = zr3[:, :, k_off:k_off + KV_W]
    v_new = zr3[:, :, k_off + KV_W:k_off + 2 * KV_W]
    return y.reshape(bsz, t_len, D_MODEL), wkv, p3[:, -1, :], k_new, v_new


def _prepare_weights(g_norm, w_in, mu_shift, w0, w_w_up, a0, w_a_up, k_k, k_a, r_k, lnx_w, lnx_b, sinks,
                     p_a, p_b, w_o, g_final):
    o_ga = SHIFT_W
    o_q = o_ga + D_A
    o_k = o_q + D_B
    o_gb = o_k + 2 * KV_W
    o_ma = o_gb + D_B
    w_shift = w_in[:, :SHIFT_W].astype(BF16)
    w_rest = jnp.concatenate([w_in[:, o_ga:o_q], w_in[:, o_gb:o_ma], w_in[:, o_ma:], w_in[:, o_q:o_gb]],
                             axis=1).astype(BF16)
    zeros = jnp.zeros((D_A,), F32)
    cparams = jnp.stack([w0, a0, k_k, k_a, r_k.reshape(D_A), lnx_w, lnx_b, zeros,
                         mu_shift[:D_A], mu_shift[D_A:2 * D_A], mu_shift[2 * D_A:3 * D_A],
                         jnp.tile(mu_shift[3 * D_A:], PAIRS), zeros, zeros, zeros, zeros])
    ww = w_w_up.reshape(LORA, PAIRS, LANES).transpose(1, 0, 2)
    wa = w_a_up.reshape(LORA, PAIRS, LANES).transpose(1, 0, 2)
    z = jnp.zeros_like(ww)
    lora_w = jnp.concatenate([jnp.concatenate([ww, z], axis=2), jnp.concatenate([z, wa], axis=2)],
                             axis=1).astype(BF16)
    return dict(g_norm=g_norm.reshape(1, D_MODEL), w_shift=w_shift, w_rest=w_rest, cparams=cparams,
                lora_w=lora_w, sinks=sinks, p_a=p_a.astype(BF16), p_b=p_b.astype(BF16),
                w_o=w_o.astype(BF16), g_final=g_final.reshape(1, D_MODEL))


def kernel(x_prompt, x_sample, state_wkv, state_shift, cache_k, cache_v, g_norm, w_in, mu_shift, w0, w_w_up,
           a0, w_a_up, k_k, k_a, r_k, lnx_w, lnx_b, sinks, p_a, p_b, w_o, g_final):
    assert g_norm.shape[0] == 1, "single-layer stack"
    wts = _prepare_weights(g_norm[0], w_in[0], mu_shift[0], w0[0], w_w_up[0], a0[0], w_a_up[0], k_k[0],
                           k_a[0], r_k[0], lnx_w[0], lnx_b[0], sinks[0], p_a[0], p_b[0], w_o[0], g_final)
    n_p, n_s = x_prompt.shape[0], x_sample.shape[0]
    cache_win = cache_k.shape[2]
    assert cache_win == WINDOW_CHUNKS * CHUNK

    y_p, wkv_p, shift_p, k_p, v_p = _layer(
        x_prompt, jnp.zeros((n_p, SHIFT_W), F32), jnp.zeros((n_p, A_HEADS, HEAD_DIM, HEAD_DIM), F32),
        None, None, wts)
    hist_k = cache_k[0].reshape(n_s, cache_win, KV_W)
    hist_v = cache_v[0].reshape(n_s, cache_win, KV_W)
    y_s, wkv_s, shift_s, k_s, v_s = _layer(x_sample, state_shift[0], state_wkv[0], hist_k, hist_v, wts)

    rows = lambda u, n: u[:, -cache_win:].reshape(n, cache_win, KV_HEADS, HEAD_DIM)[None]
    k_s = jnp.concatenate([hist_k, k_s], axis=1)
    v_s = jnp.concatenate([hist_v, v_s], axis=1)
    return (y_p, y_s,
            wkv_p[None], shift_p[None], rows(k_p, n_p), rows(v_p, n_p),
            wkv_s[None], shift_s[None], rows(k_s, n_s), rows(v_s, n_s))
```

```python
import functools

import jax
import jax.numpy as jnp
from jax import lax
from jax.experimental import pallas as pl
from jax.experimental.pallas import tpu as pltpu

F32 = jnp.float32
BF16 = jnp.bfloat16

D_MODEL = 2048
HEAD_DIM = 64
CHUNK = 64
D_A = 1024
A_HEADS = 16
LORA = 64
SHIFT_W = 3 * D_A + 2 * LORA
D_B = 1024
Q_HEADS = 16
KV_HEADS = 4
GROUP = 4
KV_W = KV_HEADS * HEAD_DIM
WINDOW_CHUNKS = 2
RMS_EPS = 1e-6
LNX_EPS = 64e-5
NEG_INF = -1e30

LANES = 128
PAIRS = D_A // LANES
REST_W = D_A + D_B + 2 * D_MODEL + D_B + 2 * KV_W
PROJ_TN = 1024
Z_W = -(-(REST_W + SHIFT_W) // PROJ_TN) * PROJ_TN
P_BLOCK = 640
P_BLOCKS = SHIFT_W // P_BLOCK
VMEM_LIMIT = 56 * 1024 * 1024


def _nn(a, b):
    return jnp.dot(a.astype(BF16), b.astype(BF16), preferred_element_type=F32)


def _nt(a, b):
    return lax.dot_general(a.astype(BF16), b.astype(BF16), (((1,), (1,)), ((), ())),
                           preferred_element_type=F32)


def _sigmoid(x):
    return 1.0 / (1.0 + jnp.exp(-x))


def _proj_kernel(x_ref, g_ref, w_ref, o_ref, h_scr):
    @pl.when(pl.program_id(1) == 0)
    def _():
        x = x_ref[...]
        h = x * lax.rsqrt(jnp.mean(x * x, axis=-1, keepdims=True) + RMS_EPS) * g_ref[...]
        h_scr[...] = h.astype(BF16)

    o_ref[...] = jnp.dot(h_scr[...], w_ref[...], preferred_element_type=F32)


def _proj(x2d, g_row, w, tm, tn):
    m, n = x2d.shape[0], w.shape[1]
    return pl.pallas_call(
        _proj_kernel,
        grid=(m // tm, n // tn),
        in_specs=[pl.BlockSpec((tm, D_MODEL), lambda i, j: (i, 0)),
                  pl.BlockSpec((1, D_MODEL), lambda i, j: (0, 0)),
                  pl.BlockSpec((D_MODEL, tn), lambda i, j: (0, j))],
        out_specs=pl.BlockSpec((tm, tn), lambda i, j: (i, j)),
        out_shape=jax.ShapeDtypeStruct((m, n), F32),
        scratch_shapes=[pltpu.VMEM((tm, D_MODEL), BF16)],
        compiler_params=pltpu.CompilerParams(dimension_semantics=("arbitrary", "arbitrary"),
                                             vmem_limit_bytes=VMEM_LIMIT),
        name="proj",
    )(x2d, g_row, w)


_W0, _A0, _KK, _KA, _RK, _LNW, _LNB, _MU_R, _MU_K, _MU_V, _MU_WA = 0, 1, 2, 3, 4, 5, 6, 8, 9, 10, 11
_CP_ROWS = 16


def _rwkv_kernel(*refs, n_sub):
    p_refs = refs[:P_BLOCKS]
    ga_ref, sh_ref, s0_ref, cp_ref, lw_ref, ya_ref, sout_ref, s_scr, carry_scr = refs[P_BLOCKS:]
    c = pl.program_id(1)
    zero64 = jnp.zeros((HEAD_DIM, HEAD_DIM), F32)

    def p_rows(rows, off):
        return p_refs[off // P_BLOCK][0, rows, off % P_BLOCK:off % P_BLOCK + LANES]

    @pl.when(c == 0)
    def _():
        for pair in range(PAIRS):
            s_scr[pair] = jnp.concatenate(
                [jnp.concatenate([s0_ref[0, 2 * pair], zero64], axis=1),
                 jnp.concatenate([zero64, s0_ref[0, 2 * pair + 1]], axis=1)], axis=0)
        carry_scr[0:1, :] = sh_ref[0]

    t_idx = lax.broadcasted_iota(jnp.int32, (CHUNK, LANES), 0)
    lane = lax.broadcasted_iota(jnp.int32, (CHUNK, LANES), 1)
    s_idx = lane & (HEAD_DIM - 1)
    low = lane < HEAD_DIM
    eye = jnp.where(s_idx == t_idx, 1.0, 0.0).astype(F32)
    row4 = lax.broadcasted_iota(jnp.int32, (2 * CHUNK, 2 * LANES), 0)
    col4 = lax.broadcasted_iota(jnp.int32, (2 * CHUNK, 2 * LANES), 1) & (HEAD_DIM - 1)
    causal = ((row4 < CHUNK) & (col4 < row4)) | ((row4 >= CHUNK) & (col4 <= row4 - CHUNK))
    r2 = lax.broadcasted_iota(jnp.int32, (LANES, LANES), 0)
    c2 = lax.broadcasted_iota(jnp.int32, (LANES, LANES), 1)
    same_head = (r2 < HEAD_DIM) == (c2 < HEAD_DIM)
    seg_ones = jnp.where(same_head, 1.0, 0.0).astype(BF16)
    rt = lax.broadcasted_iota(jnp.int32, (CHUNK, CHUNK), 0)
    ct = lax.broadcasted_iota(jnp.int32, (CHUNK, CHUNK), 1)
    tril_ones = jnp.where(ct <= rt, 1.0, 0.0).astype(BF16)

    def stack(q):
        return jnp.concatenate([jnp.where(low, q, 0.0), jnp.where(low, 0.0, q)], axis=0).astype(BF16)

    def seg_sum(xs):
        tall = jnp.concatenate([x.astype(BF16) for x in xs], axis=0)
        out = jnp.dot(tall, seg_ones, preferred_element_type=F32)
        return [out[i * CHUNK:(i + 1) * CHUNK] for i in range(len(xs))]

    wa_off = 3 * D_A
    each = range(PAIRS)
    pair_cols = [slice(pair * LANES, (pair + 1) * LANES) for pair in each]

    for ci in range(n_sub):
        rows = slice(ci * CHUNK, (ci + 1) * CHUNK)
        row = lambda r, pair: cp_ref[r:r + 1, pair_cols[pair]]

        def shifted(off, mu):
            p = p_rows(rows, off)
            if ci == 0:
                last = carry_scr[0:1, off:off + LANES]
            else:
                last = p_rows(slice(ci * CHUNK - 1, ci * CHUNK), off)
            prev = jnp.where(t_idx == 0, last, pltpu.roll(p, 1, 0))
            return p + mu * (prev - p)

        r = [shifted(i * LANES, row(_MU_R, i)) for i in each]
        k_raw = [shifted(D_A + i * LANES, row(_MU_K, i)) for i in each]
        v = [shifted(2 * D_A + i * LANES, row(_MU_V, i)) for i in each]
        xwa = shifted(wa_off, row(_MU_WA, 0))
        lora_in = jnp.where(low, jnp.tanh(xwa), xwa)

        lora = [_nn(lora_in, lw_ref[i]) for i in each]
        neg = [-(row(_W0, i) + lora[i][:, :LANES]) for i in each]
        softplus = [jnp.maximum(neg[i], 0.0) + jnp.log(1.0 + jnp.exp(-jnp.abs(neg[i]))) for i in each]
        log_decay = [-jnp.exp(-softplus[i] - 0.5) for i in each]
        a_lr = [_sigmoid(row(_A0, i) + lora[i][:, LANES:]) for i in each]

        kk = [k_raw[i] * row(_KK, i) for i in each]
        kk_sq = seg_sum([kk[i] * kk[i] for i in each])
        kk = [kk[i] / jnp.maximum(jnp.sqrt(kk_sq[i]), 1e-12) for i in each]
        k = [k_raw[i] * (1.0 + (a_lr[i] - 1.0) * row(_KA, i)) for i in each]
        b_vec = [kk[i] * a_lr[i] for i in each]

        def prefix_sum(ld):
            hi = ld.astype(BF16)
            res = ld - hi.astype(F32)
            mid = res.astype(BF16)
            lo = (res - mid.astype(F32)).astype(BF16)
            return (jnp.dot(tril_ones, hi, preferred_element_type=F32)
                    + jnp.dot(tril_ones, mid, preferred_element_type=F32)
                    + jnp.dot(tril_ones, lo, preferred_element_type=F32))

        cum = [prefix_sum(log_decay[i]) for i in each]
        cum_last = [cum[i][CHUNK - 1:CHUNK, :] for i in each]
        g_inv = [jnp.exp(-cum[i]) for i in each]
        g_tail = [jnp.exp(cum_last[i] - cum[i]) for i in each]
        a_t = [-kk[i] * jnp.exp(cum[i] - log_decay[i]) for i in each]
        r_t = [r[i] * jnp.exp(cum[i]) for i in each]
        stk_b = [stack(b_vec[i] * g_inv[i]) for i in each]
        stk_k = [stack(k[i] * g_inv[i]) for i in each]
        stk_v = [stack(v[i]) for i in each]

        ar = [jnp.concatenate([a_t[i], r_t[i]], axis=0).astype(BF16) for i in each]
        scores = [jnp.where(causal, _nt(ar[i], jnp.concatenate([stk_b[i], stk_k[i]], axis=0)), 0.0)
                  for i in each]
        a_ab = [scores[i][:CHUNK, :LANES] for i in each]
        a_rb = [scores[i][CHUNK:, :LANES] for i in each]
        a_xk = [scores[i][:, LANES:] for i in each]

        power = [_nn(a_ab[i], stack(a_ab[i])) for i in each]
        inv = [eye + a_ab[i] for i in each]
        for _ in range(4):
            both = [_nn(jnp.concatenate([power[i], inv[i]], axis=0), stack(power[i])) for i in each]
            power = [both[i][:CHUNK] for i in each]
            inv = [inv[i] + both[i][CHUNK:] for i in each]
        inv = [inv[i] + _nn(inv[i], stack(power[i])) for i in each]

        s_prev = [s_scr[i] for i in each]
        from_state = [_nt(ar[i], s_prev[i]) for i in each]
        from_v = [_nn(a_xk[i], stk_v[i]) for i in each]
        x = [from_state[i][:CHUNK] + from_v[i][:CHUNK] for i in each]
        u = [_nn(inv[i], stack(x[i])) for i in each]
        y = [from_state[i][CHUNK:] + from_v[i][CHUNK:] + _nn(a_rb[i], stack(u[i])) for i in each]
        for i in each:
            uv = jnp.concatenate([u[i], v[i]], axis=0)
            bk = jnp.concatenate([b_vec[i] * g_tail[i], k[i] * g_tail[i]], axis=0)
            s_scr[i] = s_prev[i] * jnp.exp(cum_last[i]) + jnp.where(same_head, _nn(uv.T, bk), 0.0)

        y_sum = seg_sum(y)
        d = [y[i] - y_sum[i] * (1.0 / HEAD_DIM) for i in each]
        d_sq = seg_sum([d[i] * d[i] for i in each])
        bonus = seg_sum([r[i] * k[i] * row(_RK, i) for i in each])
        for i in each:
            var = d_sq[i] * (1.0 / HEAD_DIM)
            yn = d[i] * lax.rsqrt(var + LNX_EPS) * row(_LNW, i) + row(_LNB, i) + bonus[i] * v[i]
            gate = ga_ref[0, rows, pair_cols[i]]
            ya_ref[0, rows, pair_cols[i]] = (yn * (gate * _sigmoid(gate))).astype(BF16)

    for j in range(P_BLOCKS):
        carry_scr[0:1, j * P_BLOCK:(j + 1) * P_BLOCK] = p_refs[j][0, n_sub * CHUNK - 1:n_sub * CHUNK, :]

    @pl.when(c == pl.num_programs(1) - 1)
    def _():
        for pair in range(PAIRS):
            s = s_scr[pair]
            sout_ref[0, 2 * pair] = s[:HEAD_DIM, :HEAD_DIM]
            sout_ref[0, 2 * pair + 1] = s[HEAD_DIM:, HEAD_DIM:]


def _rwkv(z3, shift0, wkv0, cparams, lora_w, n_sub):
    bsz, t_len, _ = z3.shape
    tb = n_sub * CHUNK
    state_spec = pl.BlockSpec((1, A_HEADS, HEAD_DIM, HEAD_DIM), lambda b, c: (b, 0, 0, 0))
    return pl.pallas_call(
        functools.partial(_rwkv_kernel, n_sub=n_sub),
        grid=(bsz, t_len // tb),
        in_specs=[pl.BlockSpec((1, tb, P_BLOCK), lambda b, c, j=j: (b, c, REST_W // P_BLOCK + j))
                  for j in range(P_BLOCKS)]
                 + [pl.BlockSpec((1, tb, D_A), lambda b, c: (b, c, 0)),
                  pl.BlockSpec((1, 1, SHIFT_W), lambda b, c: (b, 0, 0)),
                  state_spec,
                  pl.BlockSpec((_CP_ROWS, D_A), lambda b, c: (0, 0)),
                  pl.BlockSpec((PAIRS, LANES, 2 * LANES), lambda b, c: (0, 0, 0))],
        out_specs=[pl.BlockSpec((1, tb, D_A), lambda b, c: (b, c, 0)), state_spec],
        out_shape=[jax.ShapeDtypeStruct((bsz, t_len, D_A), BF16),
                   jax.ShapeDtypeStruct((bsz, A_HEADS, HEAD_DIM, HEAD_DIM), F32)],
        scratch_shapes=[pltpu.VMEM((PAIRS, LANES, LANES), F32), pltpu.VMEM((8, SHIFT_W), F32)],
        compiler_params=pltpu.CompilerParams(dimension_semantics=("arbitrary", "arbitrary")),
        name="rwkv",
    )(*([z3] * P_BLOCKS), z3, shift0, wkv0, cparams, lora_w)


_SLOPES = tuple(2.0 ** (-8.0 * (h + 1) / Q_HEADS) for h in range(Q_HEADS))


HIST = WINDOW_CHUNKS * CHUNK
HEAD_BATCH = 8


def _attn_kernel(sinks_ref, q_ref, kh_ref, kc_ref, vh_ref, vc_ref, gb_ref, yb_ref, bias_scr, *, masked_history):
    cb = pl.program_id(1)
    n_q = q_ref.shape[1]
    n_k = HIST + n_q

    @pl.when((pl.program_id(0) == 0) & (cb == 0))
    def _():
        qi = lax.broadcasted_iota(jnp.int32, (n_q, n_k), 0)
        kj = lax.broadcasted_iota(jnp.int32, (n_q, n_k), 1)
        dist = jnp.abs(qi + HIST - kj).astype(F32)
        shift = CHUNK.bit_length() - 1
        back = lax.shift_right_logical(qi, shift) + WINDOW_CHUNKS - lax.shift_right_logical(kj, shift)
        visible = (back >= 0) & (back <= WINDOW_CHUNKS)
        for head in range(Q_HEADS):
            bias_scr[head] = jnp.where(visible, -_SLOPES[head] * dist, NEG_INF)
            if masked_history:
                bias_scr[Q_HEADS + head] = jnp.where(visible & (kj >= HIST), -_SLOPES[head] * dist, NEG_INF)

    first = jnp.where(cb == 0, Q_HEADS, 0) if masked_history else 0
    q = q_ref[0] * (HEAD_DIM ** -0.5)
    k_all = jnp.concatenate([kh_ref[0], kc_ref[0]], axis=0)
    v_all = jnp.concatenate([vh_ref[0], vc_ref[0]], axis=0)
    low_k = lax.broadcasted_iota(jnp.int32, (n_k, LANES), 1) < HEAD_DIM
    low_q = lax.broadcasted_iota(jnp.int32, (n_q, LANES), 1) < HEAD_DIM
    ones = jnp.ones((n_k, LANES), BF16)

    def both_halves(pair, head_is_low):
        swapped = pltpu.roll(pair, HEAD_DIM, 1)
        return jnp.where(low_k, pair, swapped) if head_is_low else jnp.where(low_k, swapped, pair)

    heads = range(Q_HEADS)
    k_dup, pv_rhs = [], []
    for h in range(KV_HEADS):
        lanes = slice((h // 2) * LANES, (h // 2 + 1) * LANES)
        k_dup.append(both_halves(k_all[:, lanes], h % 2 == 0).astype(BF16))
        v_dup = both_halves(v_all[:, lanes], h % 2 == 0).astype(BF16)
        pv_rhs.append(jnp.concatenate([v_dup, ones], axis=1))

    def one_head(head):
        pair = q[:, (head // 2) * LANES:(head // 2 + 1) * LANES]
        return jnp.where(low_q, pair, 0.0) if head % 2 == 0 else jnp.where(low_q, 0.0, pair)

    for first_head in range(0, Q_HEADS, HEAD_BATCH):
        batch = range(first_head, first_head + HEAD_BATCH)
        s = {i: _nt(one_head(i), k_dup[i // GROUP]) + bias_scr[first + i] for i in batch}
        m = {i: jnp.maximum(jnp.max(s[i], axis=-1, keepdims=True), sinks_ref[i]) for i in batch}
        e = {i: jnp.exp(s[i] - m[i]) for i in batch}
        pv = {i: _nn(e[i], pv_rhs[i // GROUP]) for i in batch}
        o = {i: pv[i][:, :LANES] / (pv[i][:, LANES:] + jnp.exp(sinks_ref[i] - m[i])) for i in batch}
        for j in range(first_head // 2, (first_head + HEAD_BATCH) // 2):
            lanes = slice(j * LANES, (j + 1) * LANES)
            gate = gb_ref[0, :, lanes]
            yb_ref[0, :, lanes] = (jnp.where(low_q, o[2 * j], o[2 * j + 1])
                                   * (gate * _sigmoid(gate))).astype(BF16)


def _attn(zr3, sinks, hist_k, hist_v, n_q):
    bsz, t_len, _ = zr3.shape
    q_col = (2 * D_A + 2 * D_MODEL) // D_B
    k_col = (2 * D_A + 2 * D_MODEL + D_B) // KV_W
    v_col = k_col + 1
    masked_history = hist_k is None
    if masked_history:
        assert n_q == HIST
        hist_spec = lambda col: pl.BlockSpec((1, HIST, KV_W), lambda b, c: (b, jnp.maximum(c - 1, 0), col))
        hist_specs = [hist_spec(k_col), hist_spec(v_col)]
        hist_k = hist_v = zr3
    else:
        assert t_len == n_q
        hist_specs = [pl.BlockSpec((1, HIST, KV_W), lambda b, c: (b, 0, 0))] * 2
    cur = lambda col: pl.BlockSpec((1, n_q, KV_W), lambda b, c: (b, c, col))
    return pl.pallas_call(
        functools.partial(_attn_kernel, masked_history=masked_history),
        grid=(bsz, t_len // n_q),
        in_specs=[pl.BlockSpec(memory_space=pltpu.SMEM),
                  pl.BlockSpec((1, n_q, D_B), lambda b, c: (b, c, q_col)),
                  hist_specs[0], cur(k_col), hist_specs[1], cur(v_col),
                  pl.BlockSpec((1, n_q, D_B), lambda b, c: (b, c, 1))],
        out_specs=pl.BlockSpec((1, n_q, D_B), lambda b, c: (b, c, 0)),
        out_shape=jax.ShapeDtypeStruct((bsz, t_len, D_B), BF16),
        scratch_shapes=[pltpu.VMEM(((2 if masked_history else 1) * Q_HEADS, n_q, HIST + n_q), F32)],
        compiler_params=pltpu.CompilerParams(dimension_semantics=("arbitrary", "arbitrary")),
        name="attn",
    )(sinks, zr3, hist_k, zr3, hist_v, zr3, zr3)


def _merge_kernel(ya_ref, yb_ref, ma_ref, mb_ref, x_ref, pa_ref, pb_ref, wo_ref, gf_ref, o_ref):
    ua = jnp.dot(ya_ref[...], pa_ref[...], preferred_element_type=F32)
    ub = jnp.dot(yb_ref[...], pb_ref[...], preferred_element_type=F32)
    merged = _sigmoid(ma_ref[...]) * ua + _sigmoid(mb_ref[...]) * ub
    o = x_ref[...] + jnp.dot(merged.astype(BF16), wo_ref[...], preferred_element_type=F32)
    o_ref[...] = o * lax.rsqrt(jnp.mean(o * o, axis=-1, keepdims=True) + RMS_EPS) * gf_ref[...]


def _merge(ya, yb, zr, x2d, pa, pb, wo, gf_row, tm):
    m = x2d.shape[0]
    whole = lambda shape: pl.BlockSpec(shape, lambda i: (0, 0), pipeline_mode=pl.Buffered(1))
    return pl.pallas_call(
        _merge_kernel,
        grid=(m // tm,),
        in_specs=[pl.BlockSpec((tm, D_A), lambda i: (i, 0)),
                  pl.BlockSpec((tm, D_B), lambda i: (i, 0)),
                  pl.BlockSpec((tm, D_MODEL), lambda i: (i, 1)),
                  pl.BlockSpec((tm, D_MODEL), lambda i: (i, 2)),
                  pl.BlockSpec((tm, D_MODEL), lambda i: (i, 0)),
                  whole((D_A, D_MODEL)), whole((D_B, D_MODEL)), whole((D_MODEL, D_MODEL)),
                  whole((1, D_MODEL))],
        out_specs=pl.BlockSpec((tm, D_MODEL), lambda i: (i, 0)),
        out_shape=jax.ShapeDtypeStruct((m, D_MODEL), F32),
        compiler_params=pltpu.CompilerParams(dimension_semantics=("arbitrary",),
                                             vmem_limit_bytes=VMEM_LIMIT),
        name="merge",
    )(ya, yb, zr, zr, x2d, pa, pb, wo, gf_row)


def _layer(x, shift0, wkv0, hist_k, hist_v, wts):
    bsz, t_len, _ = x.shape
    m = bsz * t_len
    x2d = x.reshape(m, D_MODEL)
    zr = _proj(x2d, wts["g_norm"], wts["w_all"], tm=1024, tn=PROJ_TN)
    zr3 = zr.reshape(bsz, t_len, Z_W)
    ya, wkv = _rwkv(zr3, shift0.reshape(bsz, 1, SHIFT_W), wkv0, wts["cparams"], wts["lora_w"], n_sub=1)
    yb = _attn(zr3, wts["sinks"], hist_k, hist_v, n_q=HIST if hist_k is None else t_len)
    y = _merge(ya.reshape(m, D_A), yb.reshape(m, D_B), zr, x2d, wts["p_a"], wts["p_b"], wts["w_o"],
               wts["g_final"], tm=256)
    k_off = 2 * D_A + 2 * D_MODEL + D_B
    k_new = zr3[:, :, k_off:k_off + KV_W]
    v_new = zr3[:, :, k_off + KV_W:k_off + 2 * KV_W]
    return y.reshape(bsz, t_len, D_MODEL), wkv, zr3[:, -1, REST_W:REST_W + SHIFT_W], k_new, v_new


def _prepare_weights(g_norm, w_in, mu_shift, w0, w_w_up, a0, w_a_up, k_k, k_a, r_k, lnx_w, lnx_b, sinks,
                     p_a, p_b, w_o, g_final):
    o_ga = SHIFT_W
    o_q = o_ga + D_A
    o_k = o_q + D_B
    o_gb = o_k + 2 * KV_W
    o_ma = o_gb + D_B
    w_all = jnp.concatenate([w_in[:, o_ga:o_q], w_in[:, o_gb:o_ma], w_in[:, o_ma:], w_in[:, o_q:o_gb],
                             w_in[:, :SHIFT_W], jnp.zeros((D_MODEL, Z_W - REST_W - SHIFT_W), w_in.dtype)],
                            axis=1).astype(BF16)
    zeros = jnp.zeros((D_A,), F32)
    cparams = jnp.stack([w0, a0, k_k, k_a, r_k.reshape(D_A), lnx_w, lnx_b, zeros,
                         mu_shift[:D_A], mu_shift[D_A:2 * D_A], mu_shift[2 * D_A:3 * D_A],
                         jnp.tile(mu_shift[3 * D_A:], PAIRS), zeros, zeros, zeros, zeros])
    ww = w_w_up.reshape(LORA, PAIRS, LANES).transpose(1, 0, 2)
    wa = w_a_up.reshape(LORA, PAIRS, LANES).transpose(1, 0, 2)
    z = jnp.zeros_like(ww)
    lora_w = jnp.concatenate([jnp.concatenate([ww, z], axis=2), jnp.concatenate([z, wa], axis=2)],
                             axis=1).astype(BF16)
    return dict(g_norm=g_norm.reshape(1, D_MODEL), w_all=w_all, cparams=cparams,
                lora_w=lora_w, sinks=sinks, p_a=p_a.astype(BF16), p_b=p_b.astype(BF16),
                w_o=w_o.astype(BF16), g_final=g_final.reshape(1, D_MODEL))


def kernel(x_prompt, x_sample, state_wkv, state_shift, cache_k, cache_v, g_norm, w_in, mu_shift, w0, w_w_up,
           a0, w_a_up, k_k, k_a, r_k, lnx_w, lnx_b, sinks, p_a, p_b, w_o, g_final):
    assert g_norm.shape[0] == 1, "single-layer stack"
    wts = _prepare_weights(g_norm[0], w_in[0], mu_shift[0], w0[0], w_w_up[0], a0[0], w_a_up[0], k_k[0],
                           k_a[0], r_k[0], lnx_w[0], lnx_b[0], sinks[0], p_a[0], p_b[0], w_o[0], g_final)
    n_p, n_s = x_prompt.shape[0], x_sample.shape[0]
    cache_win = cache_k.shape[2]
    assert cache_win == WINDOW_CHUNKS * CHUNK

    y_p, wkv_p, shift_p, k_p, v_p = _layer(
        x_prompt, jnp.zeros((n_p, SHIFT_W), F32), jnp.zeros((n_p, A_HEADS, HEAD_DIM, HEAD_DIM), F32),
        None, None, wts)
    hist_k = cache_k[0].reshape(n_s, cache_win, KV_W)
    hist_v = cache_v[0].reshape(n_s, cache_win, KV_W)
    y_s, wkv_s, shift_s, k_s, v_s = _layer(x_sample, state_shift[0], state_wkv[0], hist_k, hist_v, wts)

    rows = lambda u, n: u[:, -cache_win:].reshape(n, cache_win, KV_HEADS, HEAD_DIM)[None]
    k_s = jnp.concatenate([hist_k, k_s], axis=1)
    v_s = jnp.concatenate([hist_v, v_s], axis=1)
    return (y_p, y_s,
            wkv_p[None], shift_p[None], rows(k_p, n_p), rows(v_p, n_p),
            wkv_s[None], shift_s[None], rows(k_s, n_s), rows(v_s, n_s))
```

```python
import functools

import jax
import jax.numpy as jnp
from jax import lax
from jax.experimental import pallas as pl
from jax.experimental.pallas import tpu as pltpu

F32 = jnp.float32
BF16 = jnp.bfloat16

D_MODEL = 2048
HEAD_DIM = 64
CHUNK = 64
D_A = 1024
A_HEADS = 16
LORA = 64
SHIFT_W = 3 * D_A + 2 * LORA
D_B = 1024
Q_HEADS = 16
KV_HEADS = 4
GROUP = 4
KV_W = KV_HEADS * HEAD_DIM
WINDOW_CHUNKS = 2
RMS_EPS = 1e-6
LNX_EPS = 64e-5
NEG_INF = -1e30

LANES = 128
PAIRS = D_A // LANES
REST_W = D_A + D_B + 2 * D_MODEL + D_B + 2 * KV_W
PROJ_TN = 1024
Z_W = -(-(REST_W + SHIFT_W) // PROJ_TN) * PROJ_TN
P_BLOCK = 640
P_BLOCKS = SHIFT_W // P_BLOCK
VMEM_LIMIT = 56 * 1024 * 1024


def _nn(a, b):
    return jnp.dot(a.astype(BF16), b.astype(BF16), preferred_element_type=F32)


def _nt(a, b):
    return lax.dot_general(a.astype(BF16), b.astype(BF16), (((1,), (1,)), ((), ())),
                           preferred_element_type=F32)


def _sigmoid(x):
    return 1.0 / (1.0 + jnp.exp(-x))


def _proj_kernel(x_ref, g_ref, w_ref, o_ref, h_scr):
    @pl.when(pl.program_id(1) == 0)
    def _():
        x = x_ref[...]
        h = x * lax.rsqrt(jnp.mean(x * x, axis=-1, keepdims=True) + RMS_EPS) * g_ref[...]
        h_scr[...] = h.astype(BF16)

    o_ref[...] = jnp.dot(h_scr[...], w_ref[...], preferred_element_type=F32)


def _proj(x2d, g_row, w, tm, tn):
    m, n = x2d.shape[0], w.shape[1]
    return pl.pallas_call(
        _proj_kernel,
        grid=(m // tm, n // tn),
        in_specs=[pl.BlockSpec((tm, D_MODEL), lambda i, j: (i, 0)),
                  pl.BlockSpec((1, D_MODEL), lambda i, j: (0, 0)),
                  pl.BlockSpec((D_MODEL, tn), lambda i, j: (0, j))],
        out_specs=pl.BlockSpec((tm, tn), lambda i, j: (i, j)),
        out_shape=jax.ShapeDtypeStruct((m, n), F32),
        scratch_shapes=[pltpu.VMEM((tm, D_MODEL), BF16)],
        compiler_params=pltpu.CompilerParams(dimension_semantics=("arbitrary", "arbitrary"),
                                             vmem_limit_bytes=VMEM_LIMIT),
        name="proj",
    )(x2d, g_row, w)


_W0, _A0, _KK, _KA, _RK, _LNW, _LNB, _MU_R, _MU_K, _MU_V, _MU_WA = 0, 1, 2, 3, 4, 5, 6, 8, 9, 10, 11
_CP_ROWS = 16


def _rwkv_kernel(*refs, n_sub, n_str):
    p_refs = refs[:P_BLOCKS]
    ga_ref, sh_ref, s0_ref, cp_ref, lw_ref, ya_ref, sout_ref, s_scr, carry_scr = refs[P_BLOCKS:]
    c = pl.program_id(1)
    zero64 = jnp.zeros((HEAD_DIM, HEAD_DIM), F32)

    def p_rows(stream, rows, off):
        return p_refs[off // P_BLOCK][stream, rows, off % P_BLOCK:off % P_BLOCK + LANES]

    @pl.when(c == 0)
    def _():
        for stream in range(n_str):
            for pair in range(PAIRS):
                s_scr[stream * PAIRS + pair] = jnp.concatenate(
                    [jnp.concatenate([s0_ref[stream, 2 * pair], zero64], axis=1),
                     jnp.concatenate([zero64, s0_ref[stream, 2 * pair + 1]], axis=1)], axis=0)
            carry_scr[stream:stream + 1, :] = sh_ref[stream]

    t_idx = lax.broadcasted_iota(jnp.int32, (CHUNK, LANES), 0)
    lane = lax.broadcasted_iota(jnp.int32, (CHUNK, LANES), 1)
    s_idx = lane & (HEAD_DIM - 1)
    low = lane < HEAD_DIM
    eye = jnp.where(s_idx == t_idx, 1.0, 0.0).astype(F32)
    row4 = lax.broadcasted_iota(jnp.int32, (2 * CHUNK, 2 * LANES), 0)
    col4 = lax.broadcasted_iota(jnp.int32, (2 * CHUNK, 2 * LANES), 1) & (HEAD_DIM - 1)
    causal = ((row4 < CHUNK) & (col4 < row4)) | ((row4 >= CHUNK) & (col4 <= row4 - CHUNK))
    r2 = lax.broadcasted_iota(jnp.int32, (LANES, LANES), 0)
    c2 = lax.broadcasted_iota(jnp.int32, (LANES, LANES), 1)
    same_head = (r2 < HEAD_DIM) == (c2 < HEAD_DIM)
    seg_ones = jnp.where(same_head, 1.0, 0.0).astype(BF16)
    rt = lax.broadcasted_iota(jnp.int32, (CHUNK, CHUNK), 0)
    ct = lax.broadcasted_iota(jnp.int32, (CHUNK, CHUNK), 1)
    tril_ones = jnp.where(ct <= rt, 1.0, 0.0).astype(BF16)

    def stack(q):
        return jnp.concatenate([jnp.where(low, q, 0.0), jnp.where(low, 0.0, q)], axis=0).astype(BF16)

    def seg_sum(xs):
        tall = jnp.concatenate([x.astype(BF16) for x in xs], axis=0)
        out = jnp.dot(tall, seg_ones, preferred_element_type=F32)
        return [out[i * CHUNK:(i + 1) * CHUNK] for i in range(len(xs))]

    wa_off = 3 * D_A
    each = range(n_str * PAIRS)
    stream_of = [i // PAIRS for i in each]
    pair_of = [i % PAIRS for i in each]
    pair_cols = [slice(pair_of[i] * LANES, (pair_of[i] + 1) * LANES) for i in each]

    for ci in range(n_sub):
        rows = slice(ci * CHUNK, (ci + 1) * CHUNK)
        row = lambda r, i: cp_ref[r:r + 1, pair_cols[i]]

        def shifted(stream, off, mu):
            p = p_rows(stream, rows, off)
            if ci == 0:
                last = carry_scr[stream:stream + 1, off:off + LANES]
            else:
                last = p_rows(stream, slice(ci * CHUNK - 1, ci * CHUNK), off)
            prev = jnp.where(t_idx == 0, last, pltpu.roll(p, 1, 0))
            return p + mu * (prev - p)

        r = [shifted(stream_of[i], pair_of[i] * LANES, row(_MU_R, i)) for i in each]
        k_raw = [shifted(stream_of[i], D_A + pair_of[i] * LANES, row(_MU_K, i)) for i in each]
        v = [shifted(stream_of[i], 2 * D_A + pair_of[i] * LANES, row(_MU_V, i)) for i in each]
        lora_in = []
        for stream in range(n_str):
            xwa = shifted(stream, wa_off, row(_MU_WA, 0))
            lora_in.append(jnp.where(low, jnp.tanh(xwa), xwa).astype(BF16))

        lora = [_nn(lora_in[stream_of[i]], lw_ref[pair_of[i]]) for i in each]
        neg = [-(row(_W0, i) + lora[i][:, :LANES]) for i in each]
        softplus = [jnp.maximum(neg[i], 0.0) + jnp.log(1.0 + jnp.exp(-jnp.abs(neg[i]))) for i in each]
        log_decay = [-jnp.exp(-softplus[i] - 0.5) for i in each]
        a_lr = [_sigmoid(row(_A0, i) + lora[i][:, LANES:]) for i in each]

        kk = [k_raw[i] * row(_KK, i) for i in each]
        kk_sq = seg_sum([kk[i] * kk[i] for i in each])
        kk = [kk[i] / jnp.maximum(jnp.sqrt(kk_sq[i]), 1e-12) for i in each]
        k = [k_raw[i] * (1.0 + (a_lr[i] - 1.0) * row(_KA, i)) for i in each]
        b_vec = [kk[i] * a_lr[i] for i in each]

        def prefix_sum(ld):
            hi = ld.astype(BF16)
            res = ld - hi.astype(F32)
            mid = res.astype(BF16)
            lo = (res - mid.astype(F32)).astype(BF16)
            return (jnp.dot(tril_ones, hi, preferred_element_type=F32)
                    + jnp.dot(tril_ones, mid, preferred_element_type=F32)
                    + jnp.dot(tril_ones, lo, preferred_element_type=F32))

        cum = [prefix_sum(log_decay[i]) for i in each]
        cum_last = [cum[i][CHUNK - 1:CHUNK, :] for i in each]
        g_inv = [jnp.exp(-cum[i]) for i in each]
        g_tail = [jnp.exp(cum_last[i] - cum[i]) for i in each]
        a_t = [-kk[i] * jnp.exp(cum[i] - log_decay[i]) for i in each]
        r_t = [r[i] * jnp.exp(cum[i]) for i in each]
        stk_b = [stack(b_vec[i] * g_inv[i]) for i in each]
        stk_k = [stack(k[i] * g_inv[i]) for i in each]
        stk_v = [stack(v[i]) for i in each]

        ar = [jnp.concatenate([a_t[i], r_t[i]], axis=0).astype(BF16) for i in each]
        scores = [jnp.where(causal, _nt(ar[i], jnp.concatenate([stk_b[i], stk_k[i]], axis=0)), 0.0)
                  for i in each]
        a_ab = [scores[i][:CHUNK, :LANES] for i in each]
        a_rb = [scores[i][CHUNK:, :LANES] for i in each]
        a_xk = [scores[i][:, LANES:] for i in each]

        power = [_nn(a_ab[i], stack(a_ab[i])) for i in each]
        inv = [eye + a_ab[i] for i in each]
        for _ in range(4):
            both = [_nn(jnp.concatenate([power[i], inv[i]], axis=0), stack(power[i])) for i in each]
            power = [both[i][:CHUNK] for i in each]
            inv = [inv[i] + both[i][CHUNK:] for i in each]
        inv = [inv[i] + _nn(inv[i], stack(power[i])) for i in each]

        s_prev = [s_scr[i] for i in each]
        from_state = [_nt(ar[i], s_prev[i]) for i in each]
        from_v = [_nn(a_xk[i], stk_v[i]) for i in each]
        x = [from_state[i][:CHUNK] + from_v[i][:CHUNK] for i in each]
        u = [_nn(inv[i], stack(x[i])) for i in each]
        y = [from_state[i][CHUNK:] + from_v[i][CHUNK:] + _nn(a_rb[i], stack(u[i])) for i in each]
        for i in each:
            uv = jnp.concatenate([u[i], v[i]], axis=0)
            bk = jnp.concatenate([b_vec[i] * g_tail[i], k[i] * g_tail[i]], axis=0)
            s_scr[i] = s_prev[i] * jnp.exp(cum_last[i]) + jnp.where(same_head, _nn(uv.T, bk), 0.0)

        y_sum = seg_sum(y)
        d = [y[i] - y_sum[i] * (1.0 / HEAD_DIM) for i in each]
        d_sq = seg_sum([d[i] * d[i] for i in each])
        bonus = seg_sum([r[i] * k[i] * row(_RK, i) for i in each])
        for i in each:
            var = d_sq[i] * (1.0 / HEAD_DIM)
            yn = d[i] * lax.rsqrt(var + LNX_EPS) * row(_LNW, i) + row(_LNB, i) + bonus[i] * v[i]
            gate = ga_ref[stream_of[i], rows, pair_cols[i]]
            ya_ref[stream_of[i], rows, pair_cols[i]] = (yn * (gate * _sigmoid(gate))).astype(BF16)

    for stream in range(n_str):
        for j in range(P_BLOCKS):
            carry_scr[stream:stream + 1, j * P_BLOCK:(j + 1) * P_BLOCK] = (
                p_refs[j][stream, n_sub * CHUNK - 1:n_sub * CHUNK, :])

    @pl.when(c == pl.num_programs(1) - 1)
    def _():
        for i in each:
            s = s_scr[i]
            sout_ref[stream_of[i], 2 * pair_of[i]] = s[:HEAD_DIM, :HEAD_DIM]
            sout_ref[stream_of[i], 2 * pair_of[i] + 1] = s[HEAD_DIM:, HEAD_DIM:]


def _rwkv(z3, shift0, wkv0, cparams, lora_w, n_sub, n_str):
    bsz, t_len, _ = z3.shape
    tb = n_sub * CHUNK
    state_spec = pl.BlockSpec((n_str, A_HEADS, HEAD_DIM, HEAD_DIM), lambda b, c: (b, 0, 0, 0))
    return pl.pallas_call(
        functools.partial(_rwkv_kernel, n_sub=n_sub, n_str=n_str),
        grid=(bsz // n_str, t_len // tb),
        in_specs=[pl.BlockSpec((n_str, tb, P_BLOCK), lambda b, c, j=j: (b, c, REST_W // P_BLOCK + j))
                  for j in range(P_BLOCKS)]
                 + [pl.BlockSpec((n_str, tb, D_A), lambda b, c: (b, c, 0)),
                  pl.BlockSpec((n_str, 1, SHIFT_W), lambda b, c: (b, 0, 0)),
                  state_spec,
                  pl.BlockSpec((_CP_ROWS, D_A), lambda b, c: (0, 0)),
                  pl.BlockSpec((PAIRS, LANES, 2 * LANES), lambda b, c: (0, 0, 0))],
        out_specs=[pl.BlockSpec((n_str, tb, D_A), lambda b, c: (b, c, 0)), state_spec],
        out_shape=[jax.ShapeDtypeStruct((bsz, t_len, D_A), BF16),
                   jax.ShapeDtypeStruct((bsz, A_HEADS, HEAD_DIM, HEAD_DIM), F32)],
        scratch_shapes=[pltpu.VMEM((n_str * PAIRS, LANES, LANES), F32), pltpu.VMEM((8, SHIFT_W), F32)],
        compiler_params=pltpu.CompilerParams(dimension_semantics=("arbitrary", "arbitrary")),
        name="rwkv",
    )(*([z3] * P_BLOCKS), z3, shift0, wkv0, cparams, lora_w)


_SLOPES = tuple(2.0 ** (-8.0 * (h + 1) / Q_HEADS) for h in range(Q_HEADS))


HIST = WINDOW_CHUNKS * CHUNK
HEAD_BATCH = 8


def _attn_kernel(sinks_ref, q_ref, kh_ref, kc_ref, vh_ref, vc_ref, gb_ref, yb_ref, bias_scr, *, masked_history):
    cb = pl.program_id(1)
    n_q = q_ref.shape[1]
    n_k = HIST + n_q

    @pl.when((pl.program_id(0) == 0) & (cb == 0))
    def _():
        qi = lax.broadcasted_iota(jnp.int32, (n_q, n_k), 0)
        kj = lax.broadcasted_iota(jnp.int32, (n_q, n_k), 1)
        dist = jnp.abs(qi + HIST - kj).astype(F32)
        shift = CHUNK.bit_length() - 1
        back = lax.shift_right_logical(qi, shift) + WINDOW_CHUNKS - lax.shift_right_logical(kj, shift)
        visible = (back >= 0) & (back <= WINDOW_CHUNKS)
        for head in range(Q_HEADS):
            bias_scr[head] = jnp.where(visible, -_SLOPES[head] * dist, NEG_INF)
            if masked_history:
                bias_scr[Q_HEADS + head] = jnp.where(visible & (kj >= HIST), -_SLOPES[head] * dist, NEG_INF)

    first = jnp.where(cb == 0, Q_HEADS, 0) if masked_history else 0
    q = q_ref[0] * (HEAD_DIM ** -0.5)
    k_all = jnp.concatenate([kh_ref[0], kc_ref[0]], axis=0)
    v_all = jnp.concatenate([vh_ref[0], vc_ref[0]], axis=0)
    low_k = lax.broadcasted_iota(jnp.int32, (n_k, LANES), 1) < HEAD_DIM
    low_q = lax.broadcasted_iota(jnp.int32, (n_q, LANES), 1) < HEAD_DIM
    ones = jnp.ones((n_k, LANES), BF16)

    def both_halves(pair, head_is_low):
        swapped = pltpu.roll(pair, HEAD_DIM, 1)
        return jnp.where(low_k, pair, swapped) if head_is_low else jnp.where(low_k, swapped, pair)

    heads = range(Q_HEADS)
    k_dup, pv_rhs = [], []
    for h in range(KV_HEADS):
        lanes = slice((h // 2) * LANES, (h // 2 + 1) * LANES)
        k_dup.append(both_halves(k_all[:, lanes], h % 2 == 0).astype(BF16))
        v_dup = both_halves(v_all[:, lanes], h % 2 == 0).astype(BF16)
        pv_rhs.append(jnp.concatenate([v_dup, ones], axis=1))

    def one_head(head):
        pair = q[:, (head // 2) * LANES:(head // 2 + 1) * LANES]
        return jnp.where(low_q, pair, 0.0) if head % 2 == 0 else jnp.where(low_q, 0.0, pair)

    for first_head in range(0, Q_HEADS, HEAD_BATCH):
        batch = range(first_head, first_head + HEAD_BATCH)
        s = {i: _nt(one_head(i), k_dup[i // GROUP]) + bias_scr[first + i] for i in batch}
        m = {i: jnp.maximum(jnp.max(s[i], axis=-1, keepdims=True), sinks_ref[i]) for i in batch}
        e = {i: jnp.exp(s[i] - m[i]) for i in batch}
        pv = {i: _nn(e[i], pv_rhs[i // GROUP]) for i in batch}
        o = {i: pv[i][:, :LANES] / (pv[i][:, LANES:] + jnp.exp(sinks_ref[i] - m[i])) for i in batch}
        for j in range(first_head // 2, (first_head + HEAD_BATCH) // 2):
            lanes = slice(j * LANES, (j + 1) * LANES)
            gate = gb_ref[0, :, lanes]
            yb_ref[0, :, lanes] = (jnp.where(low_q, o[2 * j], o[2 * j + 1])
                                   * (gate * _sigmoid(gate))).astype(BF16)


def _attn(zr3, sinks, hist_k, hist_v, n_q):
    bsz, t_len, _ = zr3.shape
    q_col = (2 * D_A + 2 * D_MODEL) // D_B
    k_col = (2 * D_A + 2 * D_MODEL + D_B) // KV_W
    v_col = k_col + 1
    masked_history = hist_k is None
    if masked_history:
        assert n_q == HIST
        hist_spec = lambda col: pl.BlockSpec((1, HIST, KV_W), lambda b, c: (b, jnp.maximum(c - 1, 0), col))
        hist_specs = [hist_spec(k_col), hist_spec(v_col)]
        hist_k = hist_v = zr3
    else:
        assert t_len == n_q
        hist_specs = [pl.BlockSpec((1, HIST, KV_W), lambda b, c: (b, 0, 0))] * 2
    cur = lambda col: pl.BlockSpec((1, n_q, KV_W), lambda b, c: (b, c, col))
    return pl.pallas_call(
        functools.partial(_attn_kernel, masked_history=masked_history),
        grid=(bsz, t_len // n_q),
        in_specs=[pl.BlockSpec(memory_space=pltpu.SMEM),
                  pl.BlockSpec((1, n_q, D_B), lambda b, c: (b, c, q_col)),
                  hist_specs[0], cur(k_col), hist_specs[1], cur(v_col),
                  pl.BlockSpec((1, n_q, D_B), lambda b, c: (b, c, 1))],
        out_specs=pl.BlockSpec((1, n_q, D_B), lambda b, c: (b, c, 0)),
        out_shape=jax.ShapeDtypeStruct((bsz, t_len, D_B), BF16),
        scratch_shapes=[pltpu.VMEM(((2 if masked_history else 1) * Q_HEADS, n_q, HIST + n_q), F32)],
        compiler_params=pltpu.CompilerParams(dimension_semantics=("arbitrary", "arbitrary")),
        name="attn",
    )(sinks, zr3, hist_k, zr3, hist_v, zr3, zr3)


def _merge_kernel(ya_ref, yb_ref, ma_ref, mb_ref, x_ref, pa_ref, pb_ref, wo_ref, gf_ref, o_ref):
    ua = jnp.dot(ya_ref[...], pa_ref[...], preferred_element_type=F32)
    ub = jnp.dot(yb_ref[...], pb_ref[...], preferred_element_type=F32)
    merged = _sigmoid(ma_ref[...]) * ua + _sigmoid(mb_ref[...]) * ub
    o = x_ref[...] + jnp.dot(merged.astype(BF16), wo_ref[...], preferred_element_type=F32)
    o_ref[...] = o * lax.rsqrt(jnp.mean(o * o, axis=-1, keepdims=True) + RMS_EPS) * gf_ref[...]


def _merge(ya, yb, zr, x2d, pa, pb, wo, gf_row, tm):
    m = x2d.shape[0]
    whole = lambda shape: pl.BlockSpec(shape, lambda i: (0, 0), pipeline_mode=pl.Buffered(1))
    return pl.pallas_call(
        _merge_kernel,
        grid=(m // tm,),
        in_specs=[pl.BlockSpec((tm, D_A), lambda i: (i, 0)),
                  pl.BlockSpec((tm, D_B), lambda i: (i, 0)),
                  pl.BlockSpec((tm, D_MODEL), lambda i: (i, 1)),
                  pl.BlockSpec((tm, D_MODEL), lambda i: (i, 2)),
                  pl.BlockSpec((tm, D_MODEL), lambda i: (i, 0)),
                  whole((D_A, D_MODEL)), whole((D_B, D_MODEL)), whole((D_MODEL, D_MODEL)),
                  whole((1, D_MODEL))],
        out_specs=pl.BlockSpec((tm, D_MODEL), lambda i: (i, 0)),
        out_shape=jax.ShapeDtypeStruct((m, D_MODEL), F32),
        compiler_params=pltpu.CompilerParams(dimension_semantics=("arbitrary",),
                                             vmem_limit_bytes=VMEM_LIMIT),
        name="merge",
    )(ya, yb, zr, zr, x2d, pa, pb, wo, gf_row)


def _layer(x, shift0, wkv0, hist_k, hist_v, wts):
    bsz, t_len, _ = x.shape
    m = bsz * t_len
    x2d = x.reshape(m, D_MODEL)
    zr = _proj(x2d, wts["g_norm"], wts["w_all"], tm=1024, tn=PROJ_TN)
    zr3 = zr.reshape(bsz, t_len, Z_W)
    ya, wkv = _rwkv(zr3, shift0.reshape(bsz, 1, SHIFT_W), wkv0, wts["cparams"], wts["lora_w"], n_sub=1, n_str=4)
    yb = _attn(zr3, wts["sinks"], hist_k, hist_v, n_q=HIST if hist_k is None else t_len)
    y = _merge(ya.reshape(m, D_A), yb.reshape(m, D_B), zr, x2d, wts["p_a"], wts["p_b"], wts["w_o"],
               wts["g_final"], tm=256)
    k_off = 2 * D_A + 2 * D_MODEL + D_B
    k_new = zr3[:, :, k_off:k_off + KV_W]
    v_new = zr3[:, :, k_off + KV_W:k_off + 2 * KV_W]
    return y.reshape(bsz, t_len, D_MODEL), wkv, zr3[:, -1, REST_W:REST_W + SHIFT_W], k_new, v_new


def _prepare_weights(g_norm, w_in, mu_shift, w0, w_w_up, a0, w_a_up, k_k, k_a, r_k, lnx_w, lnx_b, sinks,
                     p_a, p_b, w_o, g_final):
    o_ga = SHIFT_W
    o_q = o_ga + D_A
    o_k = o_q + D_B
    o_gb = o_k + 2 * KV_W
    o_ma = o_gb + D_B
    w_all = jnp.concatenate([w_in[:, o_ga:o_q], w_in[:, o_gb:o_ma], w_in[:, o_ma:], w_in[:, o_q:o_gb],
                             w_in[:, :SHIFT_W], jnp.zeros((D_MODEL, Z_W - REST_W - SHIFT_W), w_in.dtype)],
                            axis=1).astype(BF16)
    zeros = jnp.zeros((D_A,), F32)
    cparams = jnp.stack([w0, a0, k_k, k_a, r_k.reshape(D_A), lnx_w, lnx_b, zeros,
                         mu_shift[:D_A], mu_shift[D_A:2 * D_A], mu_shift[2 * D_A:3 * D_A],
                         jnp.tile(mu_shift[3 * D_A:], PAIRS), zeros, zeros, zeros, zeros])
    ww = w_w_up.reshape(LORA, PAIRS, LANES).transpose(1, 0, 2)
    wa = w_a_up.reshape(LORA, PAIRS, LANES).transpose(1, 0, 2)
    z = jnp.zeros_like(ww)
    lora_w = jnp.concatenate([jnp.concatenate([ww, z], axis=2), jnp.concatenate([z, wa], axis=2)],
                             axis=1).astype(BF16)
    return dict(g_norm=g_norm.reshape(1, D_MODEL), w_all=w_all, cparams=cparams,
                lora_w=lora_w, sinks=sinks, p_a=p_a.astype(BF16), p_b=p_b.astype(BF16),
                w_o=w_o.astype(BF16), g_final=g_final.reshape(1, D_MODEL))


def kernel(x_prompt, x_sample, state_wkv, state_shift, cache_k, cache_v, g_norm, w_in, mu_shift, w0, w_w_up,
           a0, w_a_up, k_k, k_a, r_k, lnx_w, lnx_b, sinks, p_a, p_b, w_o, g_final):
    assert g_norm.shape[0] == 1, "single-layer stack"
    wts = _prepare_weights(g_norm[0], w_in[0], mu_shift[0], w0[0], w_w_up[0], a0[0], w_a_up[0], k_k[0],
                           k_a[0], r_k[0], lnx_w[0], lnx_b[0], sinks[0], p_a[0], p_b[0], w_o[0], g_final)
    n_p, n_s = x_prompt.shape[0], x_sample.shape[0]
    cache_win = cache_k.shape[2]
    assert cache_win == WINDOW_CHUNKS * CHUNK

    y_p, wkv_p, shift_p, k_p, v_p = _layer(
        x_prompt, jnp.zeros((n_p, SHIFT_W), F32), jnp.zeros((n_p, A_HEADS, HEAD_DIM, HEAD_DIM), F32),
        None, None, wts)
    hist_k = cache_k[0].reshape(n_s, cache_win, KV_W)
    hist_v = cache_v[0].reshape(n_s, cache_win, KV_W)
    y_s, wkv_s, shift_s, k_s, v_s = _layer(x_sample, state_shift[0], state_wkv[0], hist_k, hist_v, wts)

    rows = lambda u, n: u[:, -cache_win:].reshape(n, cache_win, KV_HEADS, HEAD_DIM)[None]
    k_s = jnp.concatenate([hist_k, k_s], axis=1)
    v_s = jnp.concatenate([hist_v, v_s], axis=1)
    return (y_p, y_s,
            wkv_p[None], shift_p[None], rows(k_p, n_p), rows(v_p, n_p),
            wkv_s[None], shift_s[None], rows(k_s, n_s), rows(v_s, n_s))
```

```python
import functools

import jax
import jax.numpy as jnp
from jax import lax
from jax.experimental import pallas as pl
from jax.experimental.pallas import tpu as pltpu

F32 = jnp.float32
BF16 = jnp.bfloat16

D_MODEL = 2048
HEAD_DIM = 64
CHUNK = 64
D_A = 1024
A_HEADS = 16
LORA = 64
SHIFT_W = 3 * D_A + 2 * LORA
D_B = 1024
Q_HEADS = 16
KV_HEADS = 4
GROUP = 4
KV_W = KV_HEADS * HEAD_DIM
WINDOW_CHUNKS = 2
RMS_EPS = 1e-6
LNX_EPS = 64e-5
NEG_INF = -1e30

LANES = 128
PAIRS = D_A // LANES
REST_W = D_A + D_B + 2 * D_MODEL + D_B + 2 * KV_W
PROJ_TN = 1024
Z_W = -(-(REST_W + SHIFT_W) // PROJ_TN) * PROJ_TN
P_BLOCK = 640
P_BLOCKS = SHIFT_W // P_BLOCK
TAIL_W = -(-(2 * KV_W + SHIFT_W) // PROJ_TN) * PROJ_TN
VMEM_LIMIT = 56 * 1024 * 1024


def _nn(a, b):
    return jnp.dot(a.astype(BF16), b.astype(BF16), preferred_element_type=F32)


def _nt(a, b):
    return lax.dot_general(a.astype(BF16), b.astype(BF16), (((1,), (1,)), ((), ())),
                           preferred_element_type=F32)


def _sigmoid(x):
    return 1.0 / (1.0 + jnp.exp(-x))


def _proj_kernel(x_ref, g_ref, w_ref, o_ref, h_scr):
    @pl.when(pl.program_id(1) == 0)
    def _():
        x = x_ref[...]
        h = x * lax.rsqrt(jnp.mean(x * x, axis=-1, keepdims=True) + RMS_EPS) * g_ref[...]
        h_scr[...] = h.astype(BF16)

    o_ref[...] = jnp.dot(h_scr[...], w_ref[...], preferred_element_type=F32).astype(o_ref.dtype)


def _proj(x2d, g_row, w, tm, tn, out_dtype):
    m, n = x2d.shape[0], w.shape[1]
    return pl.pallas_call(
        _proj_kernel,
        grid=(m // tm, n // tn),
        in_specs=[pl.BlockSpec((tm, D_MODEL), lambda i, j: (i, 0)),
                  pl.BlockSpec((1, D_MODEL), lambda i, j: (0, 0)),
                  pl.BlockSpec((D_MODEL, tn), lambda i, j: (0, j))],
        out_specs=pl.BlockSpec((tm, tn), lambda i, j: (i, j)),
        out_shape=jax.ShapeDtypeStruct((m, n), out_dtype),
        scratch_shapes=[pltpu.VMEM((tm, D_MODEL), BF16)],
        compiler_params=pltpu.CompilerParams(dimension_semantics=("arbitrary", "arbitrary"),
                                             vmem_limit_bytes=VMEM_LIMIT),
        name="proj",
    )(x2d, g_row, w)


_W0, _A0, _KK, _KA, _RK, _LNW, _LNB, _MU_R, _MU_K, _MU_V, _MU_WA = 0, 1, 2, 3, 4, 5, 6, 8, 9, 10, 11
_CP_ROWS = 16


def _rwkv_kernel(*refs, n_sub, n_str):
    p_refs = refs[:P_BLOCKS]
    ga_ref, sh_ref, s0_ref, cp_ref, lw_ref, ya_ref, sout_ref, s_scr, carry_scr = refs[P_BLOCKS:]
    c = pl.program_id(1)
    zero64 = jnp.zeros((HEAD_DIM, HEAD_DIM), F32)

    def p_rows(stream, rows, off):
        return p_refs[off // P_BLOCK][stream, rows, off % P_BLOCK:off % P_BLOCK + LANES].astype(F32)

    @pl.when(c == 0)
    def _():
        for stream in range(n_str):
            for pair in range(PAIRS):
                s_scr[stream * PAIRS + pair] = jnp.concatenate(
                    [jnp.concatenate([s0_ref[stream, 2 * pair], zero64], axis=1),
                     jnp.concatenate([zero64, s0_ref[stream, 2 * pair + 1]], axis=1)], axis=0)
            carry_scr[stream:stream + 1, :] = sh_ref[stream]

    t_idx = lax.broadcasted_iota(jnp.int32, (CHUNK, LANES), 0)
    lane = lax.broadcasted_iota(jnp.int32, (CHUNK, LANES), 1)
    s_idx = lane & (HEAD_DIM - 1)
    low = lane < HEAD_DIM
    eye = jnp.where(s_idx == t_idx, 1.0, 0.0).astype(F32)
    row4 = lax.broadcasted_iota(jnp.int32, (2 * CHUNK, 2 * LANES), 0)
    col4 = lax.broadcasted_iota(jnp.int32, (2 * CHUNK, 2 * LANES), 1) & (HEAD_DIM - 1)
    causal = ((row4 < CHUNK) & (col4 < row4)) | ((row4 >= CHUNK) & (col4 <= row4 - CHUNK))
    r2 = lax.broadcasted_iota(jnp.int32, (LANES, LANES), 0)
    c2 = lax.broadcasted_iota(jnp.int32, (LANES, LANES), 1)
    same_head = (r2 < HEAD_DIM) == (c2 < HEAD_DIM)
    seg_ones = jnp.where(same_head, 1.0, 0.0).astype(BF16)
    rt = lax.broadcasted_iota(jnp.int32, (CHUNK, CHUNK), 0)
    ct = lax.broadcasted_iota(jnp.int32, (CHUNK, CHUNK), 1)
    tril_ones = jnp.where(ct <= rt, 1.0, 0.0).astype(BF16)

    def stack(q):
        return jnp.concatenate([jnp.where(low, q, 0.0), jnp.where(low, 0.0, q)], axis=0).astype(BF16)

    def seg_sum(xs):
        tall = jnp.concatenate([x.astype(BF16) for x in xs], axis=0)
        out = jnp.dot(tall, seg_ones, preferred_element_type=F32)
        return [out[i * CHUNK:(i + 1) * CHUNK] for i in range(len(xs))]

    wa_off = 3 * D_A
    each = range(n_str * PAIRS)
    stream_of = [i // PAIRS for i in each]
    pair_of = [i % PAIRS for i in each]
    pair_cols = [slice(pair_of[i] * LANES, (pair_of[i] + 1) * LANES) for i in each]

    for ci in range(n_sub):
        rows = slice(ci * CHUNK, (ci + 1) * CHUNK)
        row = lambda r, i: cp_ref[r:r + 1, pair_cols[i]]

        def shifted(stream, off, mu):
            p = p_rows(stream, rows, off)
            if ci == 0:
                last = carry_scr[stream:stream + 1, off:off + LANES]
            else:
                last = p_rows(stream, slice(ci * CHUNK - 1, ci * CHUNK), off)
            prev = jnp.where(t_idx == 0, last, pltpu.roll(p, 1, 0))
            return p + mu * (prev - p)

        r = [shifted(stream_of[i], pair_of[i] * LANES, row(_MU_R, i)) for i in each]
        k_raw = [shifted(stream_of[i], D_A + pair_of[i] * LANES, row(_MU_K, i)) for i in each]
        v = [shifted(stream_of[i], 2 * D_A + pair_of[i] * LANES, row(_MU_V, i)) for i in each]
        lora_in = []
        for stream in range(n_str):
            xwa = shifted(stream, wa_off, row(_MU_WA, 0))
            lora_in.append(jnp.where(low, jnp.tanh(xwa), xwa).astype(BF16))

        lora = [_nn(lora_in[stream_of[i]], lw_ref[pair_of[i]]) for i in each]
        neg = [-(row(_W0, i) + lora[i][:, :LANES]) for i in each]
        softplus = [jnp.maximum(neg[i], 0.0) + jnp.log(1.0 + jnp.exp(-jnp.abs(neg[i]))) for i in each]
        log_decay = [-jnp.exp(-softplus[i] - 0.5) for i in each]
        a_lr = [_sigmoid(row(_A0, i) + lora[i][:, LANES:]) for i in each]

        kk = [k_raw[i] * row(_KK, i) for i in each]
        kk_sq = seg_sum([kk[i] * kk[i] for i in each])
        kk = [kk[i] / jnp.maximum(jnp.sqrt(kk_sq[i]), 1e-12) for i in each]
        k = [k_raw[i] * (1.0 + (a_lr[i] - 1.0) * row(_KA, i)) for i in each]
        b_vec = [kk[i] * a_lr[i] for i in each]

        def prefix_sum(ld):
            hi = ld.astype(BF16)
            res = ld - hi.astype(F32)
            mid = res.astype(BF16)
            lo = (res - mid.astype(F32)).astype(BF16)
            return (jnp.dot(tril_ones, hi, preferred_element_type=F32)
                    + jnp.dot(tril_ones, mid, preferred_element_type=F32)
                    + jnp.dot(tril_ones, lo, preferred_element_type=F32))

        cum = [prefix_sum(log_decay[i]) for i in each]
        cum_last = [cum[i][CHUNK - 1:CHUNK, :] for i in each]
        g_inv = [jnp.exp(-cum[i]) for i in each]
        g_tail = [jnp.exp(cum_last[i] - cum[i]) for i in each]
        a_t = [-kk[i] * jnp.exp(cum[i] - log_decay[i]) for i in each]
        r_t = [r[i] * jnp.exp(cum[i]) for i in each]
        stk_b = [stack(b_vec[i] * g_inv[i]) for i in each]
        stk_k = [stack(k[i] * g_inv[i]) for i in each]
        stk_v = [stack(v[i]) for i in each]

        ar = [jnp.concatenate([a_t[i], r_t[i]], axis=0).astype(BF16) for i in each]
        scores = [jnp.where(causal, _nt(ar[i], jnp.concatenate([stk_b[i], stk_k[i]], axis=0)), 0.0)
                  for i in each]
        a_ab = [scores[i][:CHUNK, :LANES] for i in each]
        a_rb = [scores[i][CHUNK:, :LANES] for i in each]
        a_xk = [scores[i][:, LANES:] for i in each]

        power = [_nn(a_ab[i], stack(a_ab[i])) for i in each]
        inv = [eye + a_ab[i] for i in each]
        for _ in range(4):
            both = [_nn(jnp.concatenate([power[i], inv[i]], axis=0), stack(power[i])) for i in each]
            power = [both[i][:CHUNK] for i in each]
            inv = [inv[i] + both[i][CHUNK:] for i in each]
        inv = [inv[i] + _nn(inv[i], stack(power[i])) for i in each]

        s_prev = [s_scr[i] for i in each]
        from_state = [_nt(ar[i], s_prev[i]) for i in each]
        from_v = [_nn(a_xk[i], stk_v[i]) for i in each]
        x = [from_state[i][:CHUNK] + from_v[i][:CHUNK] for i in each]
        u = [_nn(inv[i], stack(x[i])) for i in each]
        y = [from_state[i][CHUNK:] + from_v[i][CHUNK:] + _nn(a_rb[i], stack(u[i])) for i in each]
        for i in each:
            uv = jnp.concatenate([u[i], v[i]], axis=0)
            bk = jnp.concatenate([b_vec[i] * g_tail[i], k[i] * g_tail[i]], axis=0)
            s_scr[i] = s_prev[i] * jnp.exp(cum_last[i]) + jnp.where(same_head, _nn(uv.T, bk), 0.0)

        y_sum = seg_sum(y)
        d = [y[i] - y_sum[i] * (1.0 / HEAD_DIM) for i in each]
        d_sq = seg_sum([d[i] * d[i] for i in each])
        bonus = seg_sum([r[i] * k[i] * row(_RK, i) for i in each])
        for i in each:
            var = d_sq[i] * (1.0 / HEAD_DIM)
            yn = d[i] * lax.rsqrt(var + LNX_EPS) * row(_LNW, i) + row(_LNB, i) + bonus[i] * v[i]
            gate = ga_ref[stream_of[i], rows, pair_cols[i]].astype(F32)
            ya_ref[stream_of[i], rows, pair_cols[i]] = (yn * (gate * _sigmoid(gate))).astype(BF16)

    for stream in range(n_str):
        for j in range(P_BLOCKS):
            carry_scr[stream:stream + 1, j * P_BLOCK:(j + 1) * P_BLOCK] = (
                p_refs[j][stream, n_sub * CHUNK - 1:n_sub * CHUNK, :].astype(F32))

    @pl.when(c == pl.num_programs(1) - 1)
    def _():
        for i in each:
            s = s_scr[i]
            sout_ref[stream_of[i], 2 * pair_of[i]] = s[:HEAD_DIM, :HEAD_DIM]
            sout_ref[stream_of[i], 2 * pair_of[i] + 1] = s[HEAD_DIM:, HEAD_DIM:]


def _rwkv(z3, shift0, wkv0, cparams, lora_w, n_sub, n_str):
    bsz, t_len, _ = z3.shape
    tb = n_sub * CHUNK
    state_spec = pl.BlockSpec((n_str, A_HEADS, HEAD_DIM, HEAD_DIM), lambda b, c: (b, 0, 0, 0))
    return pl.pallas_call(
        functools.partial(_rwkv_kernel, n_sub=n_sub, n_str=n_str),
        grid=(bsz // n_str, t_len // tb),
        in_specs=[pl.BlockSpec((n_str, tb, P_BLOCK), lambda b, c, j=j: (b, c, REST_W // P_BLOCK + j))
                  for j in range(P_BLOCKS)]
                 + [pl.BlockSpec((n_str, tb, D_A), lambda b, c: (b, c, 0)),
                  pl.BlockSpec((n_str, 1, SHIFT_W), lambda b, c: (b, 0, 0)),
                  state_spec,
                  pl.BlockSpec((_CP_ROWS, D_A), lambda b, c: (0, 0)),
                  pl.BlockSpec((PAIRS, LANES, 2 * LANES), lambda b, c: (0, 0, 0))],
        out_specs=[pl.BlockSpec((n_str, tb, D_A), lambda b, c: (b, c, 0)), state_spec],
        out_shape=[jax.ShapeDtypeStruct((bsz, t_len, D_A), BF16),
                   jax.ShapeDtypeStruct((bsz, A_HEADS, HEAD_DIM, HEAD_DIM), F32)],
        scratch_shapes=[pltpu.VMEM((n_str * PAIRS, LANES, LANES), F32), pltpu.VMEM((8, SHIFT_W), F32)],
        compiler_params=pltpu.CompilerParams(dimension_semantics=("arbitrary", "arbitrary")),
        name="rwkv",
    )(*([z3] * P_BLOCKS), z3, shift0, wkv0, cparams, lora_w)


_SLOPES = tuple(2.0 ** (-8.0 * (h + 1) / Q_HEADS) for h in range(Q_HEADS))


HIST = WINDOW_CHUNKS * CHUNK
HEAD_BATCH = 8


def _attn_kernel(sinks_ref, q_ref, kh_ref, kc_ref, vh_ref, vc_ref, gb_ref, yb_ref, bias_scr, *, masked_history):
    cb = pl.program_id(1)
    n_q = q_ref.shape[1]
    n_k = HIST + n_q

    @pl.when((pl.program_id(0) == 0) & (cb == 0))
    def _():
        qi = lax.broadcasted_iota(jnp.int32, (n_q, n_k), 0)
        kj = lax.broadcasted_iota(jnp.int32, (n_q, n_k), 1)
        dist = jnp.abs(qi + HIST - kj).astype(F32)
        shift = CHUNK.bit_length() - 1
        back = lax.shift_right_logical(qi, shift) + WINDOW_CHUNKS - lax.shift_right_logical(kj, shift)
        visible = (back >= 0) & (back <= WINDOW_CHUNKS)
        for head in range(Q_HEADS):
            bias_scr[head] = jnp.where(visible, -_SLOPES[head] * dist, NEG_INF)
            if masked_history:
                bias_scr[Q_HEADS + head] = jnp.where(visible & (kj >= HIST), -_SLOPES[head] * dist, NEG_INF)

    first = jnp.where(cb == 0, Q_HEADS, 0) if masked_history else 0
    q = q_ref[0].astype(F32) * (HEAD_DIM ** -0.5)
    k_all = jnp.concatenate([kh_ref[0].astype(F32), kc_ref[0].astype(F32)], axis=0)
    v_all = jnp.concatenate([vh_ref[0].astype(F32), vc_ref[0].astype(F32)], axis=0)
    low_k = lax.broadcasted_iota(jnp.int32, (n_k, LANES), 1) < HEAD_DIM
    low_q = lax.broadcasted_iota(jnp.int32, (n_q, LANES), 1) < HEAD_DIM
    ones = jnp.ones((n_k, LANES), BF16)

    def both_halves(pair, head_is_low):
        swapped = pltpu.roll(pair, HEAD_DIM, 1)
        return jnp.where(low_k, pair, swapped) if head_is_low else jnp.where(low_k, swapped, pair)

    heads = range(Q_HEADS)
    k_dup, pv_rhs = [], []
    for h in range(KV_HEADS):
        lanes = slice((h // 2) * LANES, (h // 2 + 1) * LANES)
        k_dup.append(both_halves(k_all[:, lanes], h % 2 == 0).astype(BF16))
        v_dup = both_halves(v_all[:, lanes], h % 2 == 0).astype(BF16)
        pv_rhs.append(jnp.concatenate([v_dup, ones], axis=1))

    def one_head(head):
        pair = q[:, (head // 2) * LANES:(head // 2 + 1) * LANES]
        return jnp.where(low_q, pair, 0.0) if head % 2 == 0 else jnp.where(low_q, 0.0, pair)

    for first_head in range(0, Q_HEADS, HEAD_BATCH):
        batch = range(first_head, first_head + HEAD_BATCH)
        s = {i: _nt(one_head(i), k_dup[i // GROUP]) + bias_scr[first + i] for i in batch}
        m = {i: jnp.maximum(jnp.max(s[i], axis=-1, keepdims=True), sinks_ref[i]) for i in batch}
        e = {i: jnp.exp(s[i] - m[i]) for i in batch}
        pv = {i: _nn(e[i], pv_rhs[i // GROUP]) for i in batch}
        o = {i: pv[i][:, :LANES] / (pv[i][:, LANES:] + jnp.exp(sinks_ref[i] - m[i])) for i in batch}
        for j in range(first_head // 2, (first_head + HEAD_BATCH) // 2):
            lanes = slice(j * LANES, (j + 1) * LANES)
            gate = gb_ref[0, :, lanes].astype(F32)
            yb_ref[0, :, lanes] = (jnp.where(low_q, o[2 * j], o[2 * j + 1])
                                   * (gate * _sigmoid(gate))).astype(BF16)


def _attn(zr3, sinks, hist_k, hist_v, n_q):
    bsz, t_len, _ = zr3.shape
    q_col = (2 * D_A + 2 * D_MODEL) // D_B
    k_col = (2 * D_A + 2 * D_MODEL + D_B) // KV_W
    v_col = k_col + 1
    masked_history = hist_k is None
    if masked_history:
        assert n_q == HIST
        hist_spec = lambda col: pl.BlockSpec((1, HIST, KV_W), lambda b, c: (b, jnp.maximum(c - 1, 0), col))
        hist_specs = [hist_spec(k_col), hist_spec(v_col)]
        hist_k = hist_v = zr3
    else:
        assert t_len == n_q
        hist_specs = [pl.BlockSpec((1, HIST, KV_W), lambda b, c: (b, 0, 0))] * 2
    cur = lambda col: pl.BlockSpec((1, n_q, KV_W), lambda b, c: (b, c, col))
    return pl.pallas_call(
        functools.partial(_attn_kernel, masked_history=masked_history),
        grid=(bsz, t_len // n_q),
        in_specs=[pl.BlockSpec(memory_space=pltpu.SMEM),
                  pl.BlockSpec((1, n_q, D_B), lambda b, c: (b, c, q_col)),
                  hist_specs[0], cur(k_col), hist_specs[1], cur(v_col),
                  pl.BlockSpec((1, n_q, D_B), lambda b, c: (b, c, 1))],
        out_specs=pl.BlockSpec((1, n_q, D_B), lambda b, c: (b, c, 0)),
        out_shape=jax.ShapeDtypeStruct((bsz, t_len, D_B), BF16),
        scratch_shapes=[pltpu.VMEM(((2 if masked_history else 1) * Q_HEADS, n_q, HIST + n_q), F32)],
        compiler_params=pltpu.CompilerParams(dimension_semantics=("arbitrary", "arbitrary")),
        name="attn",
    )(sinks, zr3, hist_k, zr3, hist_v, zr3, zr3)


def _merge_kernel(ya_ref, yb_ref, ma_ref, mb_ref, x_ref, pa_ref, pb_ref, wo_ref, gf_ref, o_ref):
    ua = jnp.dot(ya_ref[...], pa_ref[...], preferred_element_type=F32)
    ub = jnp.dot(yb_ref[...], pb_ref[...], preferred_element_type=F32)
    merged = _sigmoid(ma_ref[...].astype(F32)) * ua + _sigmoid(mb_ref[...].astype(F32)) * ub
    o = x_ref[...] + jnp.dot(merged.astype(BF16), wo_ref[...], preferred_element_type=F32)
    o_ref[...] = o * lax.rsqrt(jnp.mean(o * o, axis=-1, keepdims=True) + RMS_EPS) * gf_ref[...]


def _merge(ya, yb, zr, x2d, pa, pb, wo, gf_row, tm):
    m = x2d.shape[0]
    whole = lambda shape: pl.BlockSpec(shape, lambda i: (0, 0), pipeline_mode=pl.Buffered(1))
    return pl.pallas_call(
        _merge_kernel,
        grid=(m // tm,),
        in_specs=[pl.BlockSpec((tm, D_A), lambda i: (i, 0)),
                  pl.BlockSpec((tm, D_B), lambda i: (i, 0)),
                  pl.BlockSpec((tm, D_MODEL), lambda i: (i, 1)),
                  pl.BlockSpec((tm, D_MODEL), lambda i: (i, 2)),
                  pl.BlockSpec((tm, D_MODEL), lambda i: (i, 0)),
                  whole((D_A, D_MODEL)), whole((D_B, D_MODEL)), whole((D_MODEL, D_MODEL)),
                  whole((1, D_MODEL))],
        out_specs=pl.BlockSpec((tm, D_MODEL), lambda i: (i, 0)),
        out_shape=jax.ShapeDtypeStruct((m, D_MODEL), F32),
        compiler_params=pltpu.CompilerParams(dimension_semantics=("arbitrary",),
                                             vmem_limit_bytes=VMEM_LIMIT),
        name="merge",
    )(ya, yb, zr, zr, x2d, pa, pb, wo, gf_row)


def _layer(x, shift0, wkv0, hist_k, hist_v, wts):
    bsz, t_len, _ = x.shape
    m = bsz * t_len
    x2d = x.reshape(m, D_MODEL)
    zr = _proj(x2d, wts["g_norm"], wts["w_all"], tm=1024, tn=PROJ_TN, out_dtype=BF16)
    zr3 = zr.reshape(bsz, t_len, Z_W)
    ya, wkv = _rwkv(zr3, shift0.reshape(bsz, 1, SHIFT_W), wkv0, wts["cparams"], wts["lora_w"], n_sub=1, n_str=4)
    yb = _attn(zr3, wts["sinks"], hist_k, hist_v, n_q=HIST if hist_k is None else t_len)
    y = _merge(ya.reshape(m, D_A), yb.reshape(m, D_B), zr, x2d, wts["p_a"], wts["p_b"], wts["w_o"],
               wts["g_final"], tm=512)
    n_tail = min(t_len, HIST)
    x_tail = x[:, t_len - n_tail:, :].reshape(bsz * n_tail, D_MODEL)
    tail = _proj(x_tail, wts["g_norm"], wts["w_tail"], tm=1024, tn=PROJ_TN, out_dtype=F32)
    tail = tail.reshape(bsz, n_tail, TAIL_W)
    k_new = tail[:, :, :KV_W]
    v_new = tail[:, :, KV_W:2 * KV_W]
    shift_last = tail[:, -1, 2 * KV_W:2 * KV_W + SHIFT_W]
    return y.reshape(bsz, t_len, D_MODEL), wkv, shift_last, k_new, v_new


def _prepare_weights(g_norm, w_in, mu_shift, w0, w_w_up, a0, w_a_up, k_k, k_a, r_k, lnx_w, lnx_b, sinks,
                     p_a, p_b, w_o, g_final):
    o_ga = SHIFT_W
    o_q = o_ga + D_A
    o_k = o_q + D_B
    o_gb = o_k + 2 * KV_W
    o_ma = o_gb + D_B
    w_all = jnp.concatenate([w_in[:, o_ga:o_q], w_in[:, o_gb:o_ma], w_in[:, o_ma:], w_in[:, o_q:o_gb],
                             w_in[:, :SHIFT_W], jnp.zeros((D_MODEL, Z_W - REST_W - SHIFT_W), w_in.dtype)],
                            axis=1).astype(BF16)
    w_tail = jnp.concatenate([w_in[:, o_k:o_gb], w_in[:, :SHIFT_W],
                              jnp.zeros((D_MODEL, TAIL_W - 2 * KV_W - SHIFT_W), w_in.dtype)], axis=1).astype(BF16)
    zeros = jnp.zeros((D_A,), F32)
    cparams = jnp.stack([w0, a0, k_k, k_a, r_k.reshape(D_A), lnx_w, lnx_b, zeros,
                         mu_shift[:D_A], mu_shift[D_A:2 * D_A], mu_shift[2 * D_A:3 * D_A],
                         jnp.tile(mu_shift[3 * D_A:], PAIRS), zeros, zeros, zeros, zeros])
    ww = w_w_up.reshape(LORA, PAIRS, LANES).transpose(1, 0, 2)
    wa = w_a_up.reshape(LORA, PAIRS, LANES).transpose(1, 0, 2)
    z = jnp.zeros_like(ww)
    lora_w = jnp.concatenate([jnp.concatenate([ww, z], axis=2), jnp.concatenate([z, wa], axis=2)],
                             axis=1).astype(BF16)
    return dict(g_norm=g_norm.reshape(1, D_MODEL), w_all=w_all, w_tail=w_tail, cparams=cparams,
                lora_w=lora_w, sinks=sinks, p_a=p_a.astype(BF16), p_b=p_b.astype(BF16),
                w_o=w_o.astype(BF16), g_final=g_final.reshape(1, D_MODEL))


def kernel(x_prompt, x_sample, state_wkv, state_shift, cache_k, cache_v, g_norm, w_in, mu_shift, w0, w_w_up,
           a0, w_a_up, k_k, k_a, r_k, lnx_w, lnx_b, sinks, p_a, p_b, w_o, g_final):
    assert g_norm.shape[0] == 1, "single-layer stack"
    wts = _prepare_weights(g_norm[0], w_in[0], mu_shift[0], w0[0], w_w_up[0], a0[0], w_a_up[0], k_k[0],
                           k_a[0], r_k[0], lnx_w[0], lnx_b[0], sinks[0], p_a[0], p_b[0], w_o[0], g_final)
    n_p, n_s = x_prompt.shape[0], x_sample.shape[0]
    cache_win = cache_k.shape[2]
    assert cache_win == WINDOW_CHUNKS * CHUNK

    y_p, wkv_p, shift_p, k_p, v_p = _layer(
        x_prompt, jnp.zeros((n_p, SHIFT_W), F32), jnp.zeros((n_p, A_HEADS, HEAD_DIM, HEAD_DIM), F32),
        None, None, wts)
    hist_k = cache_k[0].reshape(n_s, cache_win, KV_W)
    hist_v = cache_v[0].reshape(n_s, cache_win, KV_W)
    y_s, wkv_s, shift_s, k_s, v_s = _layer(x_sample, state_shift[0], state_wkv[0], hist_k, hist_v, wts)

    rows = lambda u, n: u[:, -cache_win:].reshape(n, cache_win, KV_HEADS, HEAD_DIM)[None]
    k_s = jnp.concatenate([hist_k, k_s], axis=1)
    v_s = jnp.concatenate([hist_v, v_s], axis=1)
    return (y_p, y_s,
            wkv_p[None], shift_p[None], rows(k_p, n_p), rows(v_p, n_p),
            wkv_s[None], shift_s[None], rows(k_s, n_s), rows(v_s, n_s))
```

```python
import functools

import jax
import jax.numpy as jnp
from jax import lax
from jax.experimental import pallas as pl
from jax.experimental.pallas import tpu as pltpu

F32 = jnp.float32
BF16 = jnp.bfloat16

D_MODEL = 2048
HEAD_DIM = 64
CHUNK = 64
D_A = 1024
A_HEADS = 16
LORA = 64
SHIFT_W = 3 * D_A + 2 * LORA
D_B = 1024
Q_HEADS = 16
KV_HEADS = 4
GROUP = 4
KV_W = KV_HEADS * HEAD_DIM
WINDOW_CHUNKS = 2
RMS_EPS = 1e-6
LNX_EPS = 64e-5
NEG_INF = -1e30

LANES = 128
PAIRS = D_A // LANES
REST_W = D_A + D_B + 2 * D_MODEL + D_B + 2 * KV_W
PROJ_TN = 1024
Z_W = -(-(REST_W + SHIFT_W) // PROJ_TN) * PROJ_TN
P_BLOCK = 640
P_BLOCKS = SHIFT_W // P_BLOCK
VMEM_LIMIT = 56 * 1024 * 1024


def _nn(a, b):
    return jnp.dot(a.astype(BF16), b.astype(BF16), preferred_element_type=F32)


def _nt(a, b):
    return lax.dot_general(a.astype(BF16), b.astype(BF16), (((1,), (1,)), ((), ())),
                           preferred_element_type=F32)


def _sigmoid(x):
    return 1.0 / (1.0 + jnp.exp(-x))


def _proj_kernel(x_ref, g_ref, w_ref, o_ref, h_scr):
    @pl.when(pl.program_id(1) == 0)
    def _():
        x = x_ref[...]
        h = x * lax.rsqrt(jnp.mean(x * x, axis=-1, keepdims=True) + RMS_EPS) * g_ref[...]
        h_scr[...] = h.astype(BF16)

    o_ref[...] = jnp.dot(h_scr[...], w_ref[0], preferred_element_type=F32)


def _proj(x2d, g_row, w_tiles, tm):
    m = x2d.shape[0]
    n_tiles, _, tn = w_tiles.shape
    return pl.pallas_call(
        _proj_kernel,
        grid=(m // tm, n_tiles),
        in_specs=[pl.BlockSpec((tm, D_MODEL), lambda i, j: (i, 0)),
                  pl.BlockSpec((1, D_MODEL), lambda i, j: (0, 0)),
                  pl.BlockSpec((1, D_MODEL, tn), lambda i, j: (j, 0, 0))],
        out_specs=pl.BlockSpec((tm, tn), lambda i, j: (i, j)),
        out_shape=jax.ShapeDtypeStruct((m, n_tiles * tn), F32),
        scratch_shapes=[pltpu.VMEM((tm, D_MODEL), BF16)],
        compiler_params=pltpu.CompilerParams(dimension_semantics=("arbitrary", "arbitrary"),
                                             vmem_limit_bytes=VMEM_LIMIT),
        name="proj",
    )(x2d, g_row, w_tiles)


_W0, _A0, _KK, _KA, _RK, _LNW, _LNB, _MU_R, _MU_K, _MU_V, _MU_WA = 0, 1, 2, 3, 4, 5, 6, 8, 9, 10, 11
_CP_ROWS = 16
RWKV_SKEW = 12


def _rwkv_stages(p_refs, ga_ref, cp_ref, lw_ref, ya_ref, s_scr, carry_scr, n_sub, streams):
    def p_rows(stream, rows, off):
        return p_refs[off // P_BLOCK][stream, rows, off % P_BLOCK:off % P_BLOCK + LANES]

    t_idx = lax.broadcasted_iota(jnp.int32, (CHUNK, LANES), 0)
    lane = lax.broadcasted_iota(jnp.int32, (CHUNK, LANES), 1)
    s_idx = lane & (HEAD_DIM - 1)
    low = lane < HEAD_DIM
    eye = jnp.where(s_idx == t_idx, 1.0, 0.0).astype(F32)
    row4 = lax.broadcasted_iota(jnp.int32, (2 * CHUNK, 2 * LANES), 0)
    col4 = lax.broadcasted_iota(jnp.int32, (2 * CHUNK, 2 * LANES), 1) & (HEAD_DIM - 1)
    causal = ((row4 < CHUNK) & (col4 < row4)) | ((row4 >= CHUNK) & (col4 <= row4 - CHUNK))
    r2 = lax.broadcasted_iota(jnp.int32, (LANES, LANES), 0)
    c2 = lax.broadcasted_iota(jnp.int32, (LANES, LANES), 1)
    same_head = (r2 < HEAD_DIM) == (c2 < HEAD_DIM)
    seg_ones = jnp.where(same_head, 1.0, 0.0).astype(BF16)
    rt = lax.broadcasted_iota(jnp.int32, (CHUNK, CHUNK), 0)
    ct = lax.broadcasted_iota(jnp.int32, (CHUNK, CHUNK), 1)
    tril_ones = jnp.where(ct <= rt, 1.0, 0.0).astype(BF16)

    def stack(q):
        return jnp.concatenate([jnp.where(low, q, 0.0), jnp.where(low, 0.0, q)], axis=0).astype(BF16)

    def seg_sum(xs):
        tall = jnp.concatenate([x.astype(BF16) for x in xs], axis=0)
        out = jnp.dot(tall, seg_ones, preferred_element_type=F32)
        return [out[i * CHUNK:(i + 1) * CHUNK] for i in range(len(xs))]

    wa_off = 3 * D_A
    each = range(len(streams) * PAIRS)
    stream_of = [streams[i // PAIRS] for i in each]
    pair_of = [i % PAIRS for i in each]
    slot_of = [stream_of[i] * PAIRS + pair_of[i] for i in each]
    pair_cols = [slice(pair_of[i] * LANES, (pair_of[i] + 1) * LANES) for i in each]

    for ci in range(n_sub):
        rows = slice(ci * CHUNK, (ci + 1) * CHUNK)
        row = lambda r, i: cp_ref[r:r + 1, pair_cols[i]]

        def shifted(stream, off, mu):
            p = p_rows(stream, rows, off)
            if ci == 0:
                last = carry_scr[stream:stream + 1, off:off + LANES]
            else:
                last = p_rows(stream, slice(ci * CHUNK - 1, ci * CHUNK), off)
            prev = jnp.where(t_idx == 0, last, pltpu.roll(p, 1, 0))
            return p + mu * (prev - p)

        r = [shifted(stream_of[i], pair_of[i] * LANES, row(_MU_R, i)) for i in each]
        k_raw = [shifted(stream_of[i], D_A + pair_of[i] * LANES, row(_MU_K, i)) for i in each]
        v = [shifted(stream_of[i], 2 * D_A + pair_of[i] * LANES, row(_MU_V, i)) for i in each]
        lora_in = []
        for stream in streams:
            xwa = shifted(stream, wa_off, row(_MU_WA, 0))
            lora_in.append(jnp.where(low, jnp.tanh(xwa), xwa).astype(BF16))
        lora_in = jnp.concatenate(lora_in, axis=0)
        yield
        lora_all = [_nn(lora_in, lw_ref[pair]) for pair in range(PAIRS)]
        lora = [lora_all[pair_of[i]][(i // PAIRS) * CHUNK:(i // PAIRS + 1) * CHUNK] for i in each]
        yield
        neg = [-(row(_W0, i) + lora[i][:, :LANES]) for i in each]
        softplus = [jnp.maximum(neg[i], 0.0) + jnp.log(1.0 + jnp.exp(-jnp.abs(neg[i]))) for i in each]
        log_decay = [-jnp.exp(-softplus[i] - 0.5) for i in each]
        a_lr = [_sigmoid(row(_A0, i) + lora[i][:, LANES:]) for i in each]
        yield
        kk = [k_raw[i] * row(_KK, i) for i in each]
        kk_sq = seg_sum([kk[i] * kk[i] for i in each])
        yield
        kk = [kk[i] / jnp.maximum(jnp.sqrt(kk_sq[i]), 1e-12) for i in each]
        k = [k_raw[i] * (1.0 + (a_lr[i] - 1.0) * row(_KA, i)) for i in each]
        b_vec = [kk[i] * a_lr[i] for i in each]
        yield
        def prefix_sum(ld):
            hi = ld.astype(BF16)
            res = ld - hi.astype(F32)
            mid = res.astype(BF16)
            lo = (res - mid.astype(F32)).astype(BF16)
            return (jnp.dot(tril_ones, hi, preferred_element_type=F32)
                    + jnp.dot(tril_ones, mid, preferred_element_type=F32)
                    + jnp.dot(tril_ones, lo, preferred_element_type=F32))

        cum = [prefix_sum(log_decay[i]) for i in each]
        yield
        cum_last = [cum[i][CHUNK - 1:CHUNK, :] for i in each]
        g_inv = [jnp.exp(-cum[i]) for i in each]
        g_tail = [jnp.exp(cum_last[i] - cum[i]) for i in each]
        a_t = [-kk[i] * jnp.exp(cum[i] - log_decay[i]) for i in each]
        r_t = [r[i] * jnp.exp(cum[i]) for i in each]
        yield
        stk_b = [stack(b_vec[i] * g_inv[i]) for i in each]
        stk_k = [stack(k[i] * g_inv[i]) for i in each]
        stk_v = [stack(v[i]) for i in each]
        ar = [jnp.concatenate([a_t[i], r_t[i]], axis=0).astype(BF16) for i in each]
        yield
        scores = [jnp.where(causal, _nt(ar[i], jnp.concatenate([stk_b[i], stk_k[i]], axis=0)), 0.0)
                  for i in each]
        a_ab = [scores[i][:CHUNK, :LANES] for i in each]
        a_rb = [scores[i][CHUNK:, :LANES] for i in each]
        a_xk = [scores[i][:, LANES:] for i in each]
        yield
        power = [_nn(a_ab[i], stack(a_ab[i])) for i in each]
        inv = [eye + a_ab[i] for i in each]
        yield
        for _ in range(4):
            both = [_nn(jnp.concatenate([power[i], inv[i]], axis=0), stack(power[i])) for i in each]
            power = [both[i][:CHUNK] for i in each]
            inv = [inv[i] + both[i][CHUNK:] for i in each]
            yield
        inv = [inv[i] + _nn(inv[i], stack(power[i])) for i in each]
        yield
        s_prev = [s_scr[slot_of[i]] for i in each]
        from_state = [_nt(ar[i], s_prev[i]) for i in each]
        from_v = [_nn(a_xk[i], stk_v[i]) for i in each]
        yield
        x = [from_state[i][:CHUNK] + from_v[i][:CHUNK] for i in each]
        u = [_nn(inv[i], stack(x[i])) for i in each]
        yield
        y = [from_state[i][CHUNK:] + from_v[i][CHUNK:] + _nn(a_rb[i], stack(u[i])) for i in each]
        yield
        for i in each:
            uv = jnp.concatenate([u[i], v[i]], axis=0)
            bk = jnp.concatenate([b_vec[i] * g_tail[i], k[i] * g_tail[i]], axis=0)
            s_scr[slot_of[i]] = s_prev[i] * jnp.exp(cum_last[i]) + jnp.where(same_head, _nn(uv.T, bk), 0.0)
        yield
        y_sum = seg_sum(y)
        d = [y[i] - y_sum[i] * (1.0 / HEAD_DIM) for i in each]
        yield
        d_sq = seg_sum([d[i] * d[i] for i in each])
        bonus = seg_sum([r[i] * k[i] * row(_RK, i) for i in each])
        yield
        for i in each:
            var = d_sq[i] * (1.0 / HEAD_DIM)
            yn = d[i] * lax.rsqrt(var + LNX_EPS) * row(_LNW, i) + row(_LNB, i) + bonus[i] * v[i]
            gate = ga_ref[stream_of[i], rows, pair_cols[i]]
            ya_ref[stream_of[i], rows, pair_cols[i]] = (yn * (gate * _sigmoid(gate))).astype(BF16)
        yield

    for stream in streams:
        for j in range(P_BLOCKS):
            carry_scr[stream:stream + 1, j * P_BLOCK:(j + 1) * P_BLOCK] = (
                p_refs[j][stream, n_sub * CHUNK - 1:n_sub * CHUNK, :])


def _skewed(first, second, skew):
    for _ in range(skew):
        next(first)
    live = [first, second]
    while live:
        for gen in list(live):
            try:
                next(gen)
            except StopIteration:
                live.remove(gen)


def _rwkv_kernel(*refs, n_sub, n_str):
    p_refs = refs[:P_BLOCKS]
    ga_ref, sh_ref, s0_ref, cp_ref, lw_ref, ya_ref, sout_ref, s_scr, carry_scr = refs[P_BLOCKS:]
    c = pl.program_id(1)
    zero64 = jnp.zeros((HEAD_DIM, HEAD_DIM), F32)

    @pl.when(c == 0)
    def _():
        for stream in range(n_str):
            for pair in range(PAIRS):
                s_scr[stream * PAIRS + pair] = jnp.concatenate(
                    [jnp.concatenate([s0_ref[stream, 2 * pair], zero64], axis=1),
                     jnp.concatenate([zero64, s0_ref[stream, 2 * pair + 1]], axis=1)], axis=0)
            carry_scr[stream:stream + 1, :] = sh_ref[stream]

    half = n_str // 2
    stages = lambda streams: _rwkv_stages(p_refs, ga_ref, cp_ref, lw_ref, ya_ref, s_scr, carry_scr, n_sub, streams)
    _skewed(stages(list(range(half))), stages(list(range(half, n_str))), RWKV_SKEW)

    @pl.when(c == pl.num_programs(1) - 1)
    def _():
        for i in range(n_str * PAIRS):
            s = s_scr[i]
            sout_ref[i // PAIRS, 2 * (i % PAIRS)] = s[:HEAD_DIM, :HEAD_DIM]
            sout_ref[i // PAIRS, 2 * (i % PAIRS) + 1] = s[HEAD_DIM:, HEAD_DIM:]


def _rwkv(z3, shift0, wkv0, cparams, lora_w, n_sub, n_str):
    bsz, t_len, _ = z3.shape
    tb = n_sub * CHUNK
    state_spec = pl.BlockSpec((n_str, A_HEADS, HEAD_DIM, HEAD_DIM), lambda b, c: (b, 0, 0, 0))
    return pl.pallas_call(
        functools.partial(_rwkv_kernel, n_sub=n_sub, n_str=n_str),
        grid=(bsz // n_str, t_len // tb),
        in_specs=[pl.BlockSpec((n_str, tb, P_BLOCK), lambda b, c, j=j: (b, c, REST_W // P_BLOCK + j))
                  for j in range(P_BLOCKS)]
                 + [pl.BlockSpec((n_str, tb, D_A), lambda b, c: (b, c, 0)),
                  pl.BlockSpec((n_str, 1, SHIFT_W), lambda b, c: (b, 0, 0)),
                  state_spec,
                  pl.BlockSpec((_CP_ROWS, D_A), lambda b, c: (0, 0)),
                  pl.BlockSpec((PAIRS, LANES, 2 * LANES), lambda b, c: (0, 0, 0))],
        out_specs=[pl.BlockSpec((n_str, tb, D_A), lambda b, c: (b, c, 0)), state_spec],
        out_shape=[jax.ShapeDtypeStruct((bsz, t_len, D_A), BF16),
                   jax.ShapeDtypeStruct((bsz, A_HEADS, HEAD_DIM, HEAD_DIM), F32)],
        scratch_shapes=[pltpu.VMEM((n_str * PAIRS, LANES, LANES), F32), pltpu.VMEM((8, SHIFT_W), F32)],
        compiler_params=pltpu.CompilerParams(dimension_semantics=("arbitrary", "arbitrary")),
        name="rwkv",
    )(*([z3] * P_BLOCKS), z3, shift0, wkv0, cparams, lora_w)


_SLOPES = tuple(2.0 ** (-8.0 * (h + 1) / Q_HEADS) for h in range(Q_HEADS))


HIST = WINDOW_CHUNKS * CHUNK
HEAD_BATCH = 8


def _attn_kernel(sinks_ref, q_ref, kh_ref, kc_ref, vh_ref, vc_ref, gb_ref, yb_ref, bias_scr, *, masked_history):
    cb = pl.program_id(1)
    n_q = q_ref.shape[1]
    n_k = HIST + n_q

    @pl.when((pl.program_id(0) == 0) & (cb == 0))
    def _():
        qi = lax.broadcasted_iota(jnp.int32, (n_q, n_k), 0)
        kj = lax.broadcasted_iota(jnp.int32, (n_q, n_k), 1)
        dist = jnp.abs(qi + HIST - kj).astype(F32)
        shift = CHUNK.bit_length() - 1
        back = lax.shift_right_logical(qi, shift) + WINDOW_CHUNKS - lax.shift_right_logical(kj, shift)
        visible = (back >= 0) & (back <= WINDOW_CHUNKS)
        for head in range(Q_HEADS):
            bias_scr[head] = jnp.where(visible, -_SLOPES[head] * dist, NEG_INF)
            if masked_history:
                bias_scr[Q_HEADS + head] = jnp.where(visible & (kj >= HIST), -_SLOPES[head] * dist, NEG_INF)

    first = jnp.where(cb == 0, Q_HEADS, 0) if masked_history else 0
    q = q_ref[0] * (HEAD_DIM ** -0.5)
    k_all = jnp.concatenate([kh_ref[0], kc_ref[0]], axis=0)
    v_all = jnp.concatenate([vh_ref[0], vc_ref[0]], axis=0)
    low_k = lax.broadcasted_iota(jnp.int32, (n_k, LANES), 1) < HEAD_DIM
    low_q = lax.broadcasted_iota(jnp.int32, (n_q, LANES), 1) < HEAD_DIM
    ones = jnp.ones((n_k, LANES), BF16)

    def both_halves(pair, head_is_low):
        swapped = pltpu.roll(pair, HEAD_DIM, 1)
        return jnp.where(low_k, pair, swapped) if head_is_low else jnp.where(low_k, swapped, pair)

    k_dup, pv_rhs = [], []
    for h in range(KV_HEADS):
        lanes = slice((h // 2) * LANES, (h // 2 + 1) * LANES)
        k_dup.append(both_halves(k_all[:, lanes], h % 2 == 0).astype(BF16))
        v_dup = both_halves(v_all[:, lanes], h % 2 == 0).astype(BF16)
        pv_rhs.append(jnp.concatenate([v_dup, ones], axis=1))

    def one_head(head):
        pair = q[:, (head // 2) * LANES:(head // 2 + 1) * LANES]
        return jnp.where(low_q, pair, 0.0) if head % 2 == 0 else jnp.where(low_q, 0.0, pair)

    for first_head in range(0, Q_HEADS, HEAD_BATCH):
        batch = range(first_head, first_head + HEAD_BATCH)
        kv_heads = range(first_head // GROUP, (first_head + HEAD_BATCH) // GROUP)
        members = lambda h: range(h * GROUP, (h + 1) * GROUP)
        part = lambda tall, i: tall[(i % GROUP) * n_q:(i % GROUP + 1) * n_q]
        qk = {h: _nt(jnp.concatenate([one_head(i) for i in members(h)], axis=0), k_dup[h]) for h in kv_heads}
        s = {i: part(qk[i // GROUP], i) + bias_scr[first + i] for i in batch}
        m = {i: jnp.maximum(jnp.max(s[i], axis=-1, keepdims=True), sinks_ref[i]) for i in batch}
        e = {i: jnp.exp(s[i] - m[i]).astype(BF16) for i in batch}
        pv_tall = {h: _nn(jnp.concatenate([e[i] for i in members(h)], axis=0), pv_rhs[h]) for h in kv_heads}
        pv = {i: part(pv_tall[i // GROUP], i) for i in batch}
        o = {i: pv[i][:, :LANES] / (pv[i][:, LANES:] + jnp.exp(sinks_ref[i] - m[i])) for i in batch}
        for j in range(first_head // 2, (first_head + HEAD_BATCH) // 2):
            lanes = slice(j * LANES, (j + 1) * LANES)
            gate = gb_ref[0, :, lanes]
            yb_ref[0, :, lanes] = (jnp.where(low_q, o[2 * j], o[2 * j + 1])
                                   * (gate * _sigmoid(gate))).astype(BF16)


def _attn(zr3, sinks, hist_k, hist_v, n_q):
    bsz, t_len, _ = zr3.shape
    q_col = (2 * D_A + 2 * D_MODEL) // D_B
    k_col = (2 * D_A + 2 * D_MODEL + D_B) // KV_W
    v_col = k_col + 1
    masked_history = hist_k is None
    if masked_history:
        assert n_q == HIST
        hist_spec = lambda col: pl.BlockSpec((1, HIST, KV_W), lambda b, c: (b, jnp.maximum(c - 1, 0), col))
        hist_specs = [hist_spec(k_col), hist_spec(v_col)]
        hist_k = hist_v = zr3
    else:
        assert t_len == n_q
        hist_specs = [pl.BlockSpec((1, HIST, KV_W), lambda b, c: (b, 0, 0))] * 2
    cur = lambda col: pl.BlockSpec((1, n_q, KV_W), lambda b, c: (b, c, col))
    return pl.pallas_call(
        functools.partial(_attn_kernel, masked_history=masked_history),
        grid=(bsz, t_len // n_q),
        in_specs=[pl.BlockSpec(memory_space=pltpu.SMEM),
                  pl.BlockSpec((1, n_q, D_B), lambda b, c: (b, c, q_col)),
                  hist_specs[0], cur(k_col), hist_specs[1], cur(v_col),
                  pl.BlockSpec((1, n_q, D_B), lambda b, c: (b, c, 1))],
        out_specs=pl.BlockSpec((1, n_q, D_B), lambda b, c: (b, c, 0)),
        out_shape=jax.ShapeDtypeStruct((bsz, t_len, D_B), BF16),
        scratch_shapes=[pltpu.VMEM(((2 if masked_history else 1) * Q_HEADS, n_q, HIST + n_q), F32)],
        compiler_params=pltpu.CompilerParams(dimension_semantics=("arbitrary", "arbitrary")),
        name="attn",
    )(sinks, zr3, hist_k, zr3, hist_v, zr3, zr3)


def _merge_kernel(ya_ref, yb_ref, ma_ref, mb_ref, x_ref, pa_ref, pb_ref, wo_ref, gf_ref, o_ref):
    ua = jnp.dot(ya_ref[...], pa_ref[...], preferred_element_type=F32)
    ub = jnp.dot(yb_ref[...], pb_ref[...], preferred_element_type=F32)
    merged = _sigmoid(ma_ref[...]) * ua + _sigmoid(mb_ref[...]) * ub
    o = x_ref[...] + jnp.dot(merged.astype(BF16), wo_ref[...], preferred_element_type=F32)
    o_ref[...] = o * lax.rsqrt(jnp.mean(o * o, axis=-1, keepdims=True) + RMS_EPS) * gf_ref[...]


def _merge(ya, yb, zr, x2d, pa, pb, wo, gf_row, tm):
    m = x2d.shape[0]
    whole = lambda shape: pl.BlockSpec(shape, lambda i: (0, 0), pipeline_mode=pl.Buffered(1))
    return pl.pallas_call(
        _merge_kernel,
        grid=(m // tm,),
        in_specs=[pl.BlockSpec((tm, D_A), lambda i: (i, 0)),
                  pl.BlockSpec((tm, D_B), lambda i: (i, 0)),
                  pl.BlockSpec((tm, D_MODEL), lambda i: (i, 1)),
                  pl.BlockSpec((tm, D_MODEL), lambda i: (i, 2)),
                  pl.BlockSpec((tm, D_MODEL), lambda i: (i, 0)),
                  whole((D_A, D_MODEL)), whole((D_B, D_MODEL)), whole((D_MODEL, D_MODEL)),
                  whole((1, D_MODEL))],
        out_specs=pl.BlockSpec((tm, D_MODEL), lambda i: (i, 0)),
        out_shape=jax.ShapeDtypeStruct((m, D_MODEL), F32),
        compiler_params=pltpu.CompilerParams(dimension_semantics=("arbitrary",),
                                             vmem_limit_bytes=VMEM_LIMIT),
        name="merge",
    )(ya, yb, zr, zr, x2d, pa, pb, wo, gf_row)


def _layer(x, shift0, wkv0, hist_k, hist_v, wts):
    bsz, t_len, _ = x.shape
    m = bsz * t_len
    x2d = x.reshape(m, D_MODEL)
    zr = _proj(x2d, wts["g_norm"], wts["w_all"], tm=1024)
    zr3 = zr.reshape(bsz, t_len, Z_W)
    ya, wkv = _rwkv(zr3, shift0.reshape(bsz, 1, SHIFT_W), wkv0, wts["cparams"], wts["lora_w"], n_sub=1, n_str=4)
    yb = _attn(zr3, wts["sinks"], hist_k, hist_v, n_q=HIST if hist_k is None else t_len)
    y = _merge(ya.reshape(m, D_A), yb.reshape(m, D_B), zr, x2d, wts["p_a"], wts["p_b"], wts["w_o"],
               wts["g_final"], tm=256)
    k_off = 2 * D_A + 2 * D_MODEL + D_B
    tail = zr3[:, -min(t_len, HIST):, :]
    k_new = tail[:, :, k_off:k_off + KV_W]
    v_new = tail[:, :, k_off + KV_W:k_off + 2 * KV_W]
    return y.reshape(bsz, t_len, D_MODEL), wkv, tail[:, -1, REST_W:REST_W + SHIFT_W], k_new, v_new


def _prepare_weights(g_norm, w_in, mu_shift, w0, w_w_up, a0, w_a_up, k_k, k_a, r_k, lnx_w, lnx_b, sinks,
                     p_a, p_b, w_o, g_final):
    o_ga = SHIFT_W
    o_q = o_ga + D_A
    o_k = o_q + D_B
    o_gb = o_k + 2 * KV_W
    o_ma = o_gb + D_B
    w_all = jnp.concatenate([w_in[:, o_ga:o_q], w_in[:, o_gb:o_ma], w_in[:, o_ma:], w_in[:, o_q:o_gb],
                             w_in[:, :SHIFT_W], jnp.zeros((D_MODEL, Z_W - REST_W - SHIFT_W), w_in.dtype)],
                            axis=1).astype(BF16)
    w_all = w_all.reshape(D_MODEL, Z_W // PROJ_TN, PROJ_TN).transpose(1, 0, 2)
    zeros = jnp.zeros((D_A,), F32)
    cparams = jnp.stack([w0, a0, k_k, k_a, r_k.reshape(D_A), lnx_w, lnx_b, zeros,
                         mu_shift[:D_A], mu_shift[D_A:2 * D_A], mu_shift[2 * D_A:3 * D_A],
                         jnp.tile(mu_shift[3 * D_A:], PAIRS), zeros, zeros, zeros, zeros])
    ww = w_w_up.reshape(LORA, PAIRS, LANES).transpose(1, 0, 2)
    wa = w_a_up.reshape(LORA, PAIRS, LANES).transpose(1, 0, 2)
    z = jnp.zeros_like(ww)
    lora_w = jnp.concatenate([jnp.concatenate([ww, z], axis=2), jnp.concatenate([z, wa], axis=2)],
                             axis=1).astype(BF16)
    return dict(g_norm=g_norm.reshape(1, D_MODEL), w_all=w_all, cparams=cparams,
                lora_w=lora_w, sinks=sinks, p_a=p_a.astype(BF16), p_b=p_b.astype(BF16),
                w_o=w_o.astype(BF16), g_final=g_final.reshape(1, D_MODEL))


def kernel(x_prompt, x_sample, state_wkv, state_shift, cache_k, cache_v, g_norm, w_in, mu_shift, w0, w_w_up,
           a0, w_a_up, k_k, k_a, r_k, lnx_w, lnx_b, sinks, p_a, p_b, w_o, g_final):
    assert g_norm.shape[0] == 1, "single-layer stack"
    wts = _prepare_weights(g_norm[0], w_in[0], mu_shift[0], w0[0], w_w_up[0], a0[0], w_a_up[0], k_k[0],
                           k_a[0], r_k[0], lnx_w[0], lnx_b[0], sinks[0], p_a[0], p_b[0], w_o[0], g_final)
    n_p, n_s = x_prompt.shape[0], x_sample.shape[0]
    cache_win = cache_k.shape[2]
    assert cache_win == WINDOW_CHUNKS * CHUNK

    y_p, wkv_p, shift_p, k_p, v_p = _layer(
        x_prompt, jnp.zeros((n_p, SHIFT_W), F32), jnp.zeros((n_p, A_HEADS, HEAD_DIM, HEAD_DIM), F32),
        None, None, wts)
    hist_k = cache_k[0].reshape(n_s, cache_win, KV_W)
    hist_v = cache_v[0].reshape(n_s, cache_win, KV_W)
    y_s, wkv_s, shift_s, k_s, v_s = _layer(x_sample, state_shift[0], state_wkv[0], hist_k, hist_v, wts)

    rows = lambda u, n: u[:, -cache_win:].reshape(n, cache_win, KV_HEADS, HEAD_DIM)[None]
    k_s = jnp.concatenate([hist_k, k_s], axis=1)
    v_s = jnp.concatenate([hist_v, v_s], axis=1)
    return (y_p, y_s,
            wkv_p[None], shift_p[None], rows(k_p, n_p), rows(v_p, n_p),
            wkv_s[None], shift_s[None], rows(k_s, n_s), rows(v_s, n_s))
```

```python
import functools

import jax
import jax.numpy as jnp
from jax import lax
from jax.experimental import pallas as pl
from jax.experimental.pallas import tpu as pltpu

F32 = jnp.float32
BF16 = jnp.bfloat16

D_MODEL = 2048
HEAD_DIM = 64
CHUNK = 64
D_A = 1024
A_HEADS = 16
LORA = 64
SHIFT_W = 3 * D_A + 2 * LORA
D_B = 1024
Q_HEADS = 16
KV_HEADS = 4
GROUP = 4
KV_W = KV_HEADS * HEAD_DIM
WINDOW_CHUNKS = 2
RMS_EPS = 1e-6
LNX_EPS = 64e-5
NEG_INF = -1e30

LANES = 128
PAIRS = D_A // LANES
REST_W = D_A + D_B + 2 * D_MODEL + D_B + 2 * KV_W
PROJ_TN = 1024
Z_W = -(-(REST_W + SHIFT_W) // PROJ_TN) * PROJ_TN
P_BLOCK = 640
P_BLOCKS = SHIFT_W // P_BLOCK
VMEM_LIMIT = 56 * 1024 * 1024


def _nn(a, b):
    return jnp.dot(a.astype(BF16), b.astype(BF16), preferred_element_type=F32)


def _nt(a, b):
    return lax.dot_general(a.astype(BF16), b.astype(BF16), (((1,), (1,)), ((), ())),
                           preferred_element_type=F32)


def _sigmoid(x):
    return 1.0 / (1.0 + jnp.exp(-x))


def _proj_kernel(x_ref, g_ref, w_ref, o_ref, h_scr):
    @pl.when(pl.program_id(1) == 0)
    def _():
        x = x_ref[...]
        h = x * lax.rsqrt(jnp.mean(x * x, axis=-1, keepdims=True) + RMS_EPS) * g_ref[...]
        h_scr[...] = h.astype(BF16)

    o_ref[...] = jnp.dot(h_scr[...], w_ref[...], preferred_element_type=F32)


def _proj(x2d, g_row, w, tm, tn):
    m, n = x2d.shape[0], w.shape[1]
    return pl.pallas_call(
        _proj_kernel,
        grid=(m // tm, n // tn),
        in_specs=[pl.BlockSpec((tm, D_MODEL), lambda i, j: (i, 0)),
                  pl.BlockSpec((1, D_MODEL), lambda i, j: (0, 0)),
                  pl.BlockSpec((D_MODEL, tn), lambda i, j: (0, j))],
        out_specs=pl.BlockSpec((tm, tn), lambda i, j: (i, j)),
        out_shape=jax.ShapeDtypeStruct((m, n), F32),
        scratch_shapes=[pltpu.VMEM((tm, D_MODEL), BF16)],
        compiler_params=pltpu.CompilerParams(dimension_semantics=("arbitrary", "arbitrary"),
                                             vmem_limit_bytes=VMEM_LIMIT),
        name="proj",
    )(x2d, g_row, w)


_W0, _A0, _KK, _KA, _RK, _LNW, _LNB, _MU_R, _MU_K, _MU_V, _MU_WA = 0, 1, 2, 3, 4, 5, 6, 8, 9, 10, 11
_CP_ROWS = 16
RWKV_GROUP = 8
RWKV_SKEW = 3


def _rwkv_stages(p_refs, ga_ref, cp_ref, lw_ref, ya_ref, s_scr, carry_scr, n_sub, items):
    def p_rows(stream, rows, off):
        return p_refs[off // P_BLOCK][stream, rows, off % P_BLOCK:off % P_BLOCK + LANES]

    t_idx = lax.broadcasted_iota(jnp.int32, (CHUNK, LANES), 0)
    lane = lax.broadcasted_iota(jnp.int32, (CHUNK, LANES), 1)
    s_idx = lane & (HEAD_DIM - 1)
    low = lane < HEAD_DIM
    eye = jnp.where(s_idx == t_idx, 1.0, 0.0).astype(F32)
    row4 = lax.broadcasted_iota(jnp.int32, (2 * CHUNK, 2 * LANES), 0)
    col4 = lax.broadcasted_iota(jnp.int32, (2 * CHUNK, 2 * LANES), 1) & (HEAD_DIM - 1)
    causal = ((row4 < CHUNK) & (col4 < row4)) | ((row4 >= CHUNK) & (col4 <= row4 - CHUNK))
    r2 = lax.broadcasted_iota(jnp.int32, (LANES, LANES), 0)
    c2 = lax.broadcasted_iota(jnp.int32, (LANES, LANES), 1)
    same_head = (r2 < HEAD_DIM) == (c2 < HEAD_DIM)
    seg_ones = jnp.where(same_head, 1.0, 0.0).astype(BF16)
    rt = lax.broadcasted_iota(jnp.int32, (CHUNK, CHUNK), 0)
    ct = lax.broadcasted_iota(jnp.int32, (CHUNK, CHUNK), 1)
    tril_ones = jnp.where(ct <= rt, 1.0, 0.0).astype(BF16)

    def stack(q):
        return jnp.concatenate([jnp.where(low, q, 0.0), jnp.where(low, 0.0, q)], axis=0).astype(BF16)

    def seg_sum(xs):
        tall = jnp.concatenate([x.astype(BF16) for x in xs], axis=0)
        out = jnp.dot(tall, seg_ones, preferred_element_type=F32)
        return [out[i * CHUNK:(i + 1) * CHUNK] for i in range(len(xs))]

    wa_off = 3 * D_A
    each = range(len(items))
    stream_of = [stream for stream, _ in items]
    pair_of = [pair for _, pair in items]
    slot_of = [stream_of[i] * PAIRS + pair_of[i] for i in each]
    pair_cols = [slice(pair_of[i] * LANES, (pair_of[i] + 1) * LANES) for i in each]

    for ci in range(n_sub):
        rows = slice(ci * CHUNK, (ci + 1) * CHUNK)
        row = lambda r, i: cp_ref[r:r + 1, pair_cols[i]]

        def shifted(stream, off, mu):
            p = p_rows(stream, rows, off)
            if ci == 0:
                last = carry_scr[stream:stream + 1, off:off + LANES]
            else:
                last = p_rows(stream, slice(ci * CHUNK - 1, ci * CHUNK), off)
            prev = jnp.where(t_idx == 0, last, pltpu.roll(p, 1, 0))
            return p + mu * (prev - p)

        r = [shifted(stream_of[i], pair_of[i] * LANES, row(_MU_R, i)) for i in each]
        k_raw = [shifted(stream_of[i], D_A + pair_of[i] * LANES, row(_MU_K, i)) for i in each]
        v = [shifted(stream_of[i], 2 * D_A + pair_of[i] * LANES, row(_MU_V, i)) for i in each]
        lora_in = {}
        for stream in sorted(set(stream_of)):
            xwa = shifted(stream, wa_off, row(_MU_WA, 0))
            lora_in[stream] = jnp.where(low, jnp.tanh(xwa), xwa).astype(BF16)
        yield
        lora = [_nn(lora_in[stream_of[i]], lw_ref[pair_of[i]]) for i in each]
        yield
        neg = [-(row(_W0, i) + lora[i][:, :LANES]) for i in each]
        softplus = [jnp.maximum(neg[i], 0.0) + jnp.log(1.0 + jnp.exp(-jnp.abs(neg[i]))) for i in each]
        log_decay = [-jnp.exp(-softplus[i] - 0.5) for i in each]
        a_lr = [_sigmoid(row(_A0, i) + lora[i][:, LANES:]) for i in each]
        yield
        kk = [k_raw[i] * row(_KK, i) for i in each]
        kk_sq = seg_sum([kk[i] * kk[i] for i in each])
        yield
        kk = [kk[i] / jnp.maximum(jnp.sqrt(kk_sq[i]), 1e-12) for i in each]
        k = [k_raw[i] * (1.0 + (a_lr[i] - 1.0) * row(_KA, i)) for i in each]
        b_vec = [kk[i] * a_lr[i] for i in each]
        yield
        def prefix_sum(ld):
            hi = ld.astype(BF16)
            res = ld - hi.astype(F32)
            mid = res.astype(BF16)
            lo = (res - mid.astype(F32)).astype(BF16)
            return (jnp.dot(tril_ones, hi, preferred_element_type=F32)
                    + jnp.dot(tril_ones, mid, preferred_element_type=F32)
                    + jnp.dot(tril_ones, lo, preferred_element_type=F32))

        cum = [prefix_sum(log_decay[i]) for i in each]
        yield
        cum_last = [cum[i][CHUNK - 1:CHUNK, :] for i in each]
        g_inv = [jnp.exp(-cum[i]) for i in each]
        g_tail = [jnp.exp(cum_last[i] - cum[i]) for i in each]
        a_t = [-kk[i] * jnp.exp(cum[i] - log_decay[i]) for i in each]
        r_t = [r[i] * jnp.exp(cum[i]) for i in each]
        yield
        stk_b = [stack(b_vec[i] * g_inv[i]) for i in each]
        stk_k = [stack(k[i] * g_inv[i]) for i in each]
        stk_v = [stack(v[i]) for i in each]
        ar = [jnp.concatenate([a_t[i], r_t[i]], axis=0).astype(BF16) for i in each]
        yield
        scores = [jnp.where(causal, _nt(ar[i], jnp.concatenate([stk_b[i], stk_k[i]], axis=0)), 0.0)
                  for i in each]
        a_ab = [scores[i][:CHUNK, :LANES] for i in each]
        a_rb = [scores[i][CHUNK:, :LANES] for i in each]
        a_xk = [scores[i][:, LANES:] for i in each]
        yield
        power = [_nn(a_ab[i], stack(a_ab[i])) for i in each]
        inv = [eye + a_ab[i] for i in each]
        yield
        for _ in range(4):
            both = [_nn(jnp.concatenate([power[i], inv[i]], axis=0), stack(power[i])) for i in each]
            power = [both[i][:CHUNK] for i in each]
            inv = [inv[i] + both[i][CHUNK:] for i in each]
            yield
        inv = [inv[i] + _nn(inv[i], stack(power[i])) for i in each]
        yield
        s_prev = [s_scr[slot_of[i]] for i in each]
        from_state = [_nt(ar[i], s_prev[i]) for i in each]
        from_v = [_nn(a_xk[i], stk_v[i]) for i in each]
        yield
        x = [from_state[i][:CHUNK] + from_v[i][:CHUNK] for i in each]
        u = [_nn(inv[i], stack(x[i])) for i in each]
        yield
        y = [from_state[i][CHUNK:] + from_v[i][CHUNK:] + _nn(a_rb[i], stack(u[i])) for i in each]
        yield
        for i in each:
            uv = jnp.concatenate([u[i], v[i]], axis=0)
            bk = jnp.concatenate([b_vec[i] * g_tail[i], k[i] * g_tail[i]], axis=0)
            s_scr[slot_of[i]] = s_prev[i] * jnp.exp(cum_last[i]) + jnp.where(same_head, _nn(uv.T, bk), 0.0)
        yield
        y_sum = seg_sum(y)
        d = [y[i] - y_sum[i] * (1.0 / HEAD_DIM) for i in each]
        yield
        d_sq = seg_sum([d[i] * d[i] for i in each])
        bonus = seg_sum([r[i] * k[i] * row(_RK, i) for i in each])
        yield
        for i in each:
            var = d_sq[i] * (1.0 / HEAD_DIM)
            yn = d[i] * lax.rsqrt(var + LNX_EPS) * row(_LNW, i) + row(_LNB, i) + bonus[i] * v[i]
            gate = ga_ref[stream_of[i], rows, pair_cols[i]]
            ya_ref[stream_of[i], rows, pair_cols[i]] = (yn * (gate * _sigmoid(gate))).astype(BF16)
        yield


def _staggered(generators, skew):
    pending = list(generators)
    live = []
    rounds = 0
    while pending or live:
        if pending and rounds % skew == 0:
            live.append(pending.pop(0))
        for gen in list(live):
            try:
                next(gen)
            except StopIteration:
                live.remove(gen)
        rounds += 1


def _rwkv_kernel(*refs, n_sub, n_str):
    p_refs = refs[:P_BLOCKS]
    ga_ref, sh_ref, s0_ref, cp_ref, lw_ref, ya_ref, sout_ref, s_scr, carry_scr = refs[P_BLOCKS:]
    c = pl.program_id(1)
    zero64 = jnp.zeros((HEAD_DIM, HEAD_DIM), F32)

    @pl.when(c == 0)
    def _():
        for stream in range(n_str):
            for pair in range(PAIRS):
                s_scr[stream * PAIRS + pair] = jnp.concatenate(
                    [jnp.concatenate([s0_ref[stream, 2 * pair], zero64], axis=1),
                     jnp.concatenate([zero64, s0_ref[stream, 2 * pair + 1]], axis=1)], axis=0)
            carry_scr[stream:stream + 1, :] = sh_ref[stream]

    items = [(stream, pair) for stream in range(n_str) for pair in range(PAIRS)]
    groups = [items[g:g + RWKV_GROUP] for g in range(0, len(items), RWKV_GROUP)]
    _staggered([_rwkv_stages(p_refs, ga_ref, cp_ref, lw_ref, ya_ref, s_scr, carry_scr, n_sub, group)
                for group in groups], RWKV_SKEW)
    for stream in range(n_str):
        for j in range(P_BLOCKS):
            carry_scr[stream:stream + 1, j * P_BLOCK:(j + 1) * P_BLOCK] = (
                p_refs[j][stream, n_sub * CHUNK - 1:n_sub * CHUNK, :])

    @pl.when(c == pl.num_programs(1) - 1)
    def _():
        for i in range(n_str * PAIRS):
            s = s_scr[i]
            sout_ref[i // PAIRS, 2 * (i % PAIRS)] = s[:HEAD_DIM, :HEAD_DIM]
            sout_ref[i // PAIRS, 2 * (i % PAIRS) + 1] = s[HEAD_DIM:, HEAD_DIM:]


def _rwkv(z3, shift0, wkv0, cparams, lora_w, n_sub, n_str):
    bsz, t_len, _ = z3.shape
    tb = n_sub * CHUNK
    state_spec = pl.BlockSpec((n_str, A_HEADS, HEAD_DIM, HEAD_DIM), lambda b, c: (b, 0, 0, 0))
    return pl.pallas_call(
        functools.partial(_rwkv_kernel, n_sub=n_sub, n_str=n_str),
        grid=(bsz // n_str, t_len // tb),
        in_specs=[pl.BlockSpec((n_str, tb, P_BLOCK), lambda b, c, j=j: (b, c, REST_W // P_BLOCK + j))
                  for j in range(P_BLOCKS)]
                 + [pl.BlockSpec((n_str, tb, D_A), lambda b, c: (b, c, 0)),
                  pl.BlockSpec((n_str, 1, SHIFT_W), lambda b, c: (b, 0, 0)),
                  state_spec,
                  pl.BlockSpec((_CP_ROWS, D_A), lambda b, c: (0, 0)),
                  pl.BlockSpec((PAIRS, LANES, 2 * LANES), lambda b, c: (0, 0, 0))],
        out_specs=[pl.BlockSpec((n_str, tb, D_A), lambda b, c: (b, c, 0)), state_spec],
        out_shape=[jax.ShapeDtypeStruct((bsz, t_len, D_A), BF16),
                   jax.ShapeDtypeStruct((bsz, A_HEADS, HEAD_DIM, HEAD_DIM), F32)],
        scratch_shapes=[pltpu.VMEM((n_str * PAIRS, LANES, LANES), F32), pltpu.VMEM((8, SHIFT_W), F32)],
        compiler_params=pltpu.CompilerParams(dimension_semantics=("arbitrary", "arbitrary")),
        name="rwkv",
    )(*([z3] * P_BLOCKS), z3, shift0, wkv0, cparams, lora_w)


_SLOPES = tuple(2.0 ** (-8.0 * (h + 1) / Q_HEADS) for h in range(Q_HEADS))


HIST = WINDOW_CHUNKS * CHUNK
HEAD_BATCH = 8


def _attn_kernel(sinks_ref, q_ref, kh_ref, kc_ref, vh_ref, vc_ref, gb_ref, yb_ref, bias_scr, *, masked_history):
    cb = pl.program_id(1)
    n_q = q_ref.shape[1]
    n_k = HIST + n_q

    @pl.when((pl.program_id(0) == 0) & (cb == 0))
    def _():
        qi = lax.broadcasted_iota(jnp.int32, (n_q, n_k), 0)
        kj = lax.broadcasted_iota(jnp.int32, (n_q, n_k), 1)
        dist = jnp.abs(qi + HIST - kj).astype(F32)
        shift = CHUNK.bit_length() - 1
        back = lax.shift_right_logical(qi, shift) + WINDOW_CHUNKS - lax.shift_right_logical(kj, shift)
        visible = (back >= 0) & (back <= WINDOW_CHUNKS)
        for head in range(Q_HEADS):
            bias_scr[head] = jnp.where(visible, -_SLOPES[head] * dist, NEG_INF)
            if masked_history:
                bias_scr[Q_HEADS + head] = jnp.where(visible & (kj >= HIST), -_SLOPES[head] * dist, NEG_INF)

    first = jnp.where(cb == 0, Q_HEADS, 0) if masked_history else 0
    q = q_ref[0] * (HEAD_DIM ** -0.5)
    k_all = jnp.concatenate([kh_ref[0], kc_ref[0]], axis=0)
    v_all = jnp.concatenate([vh_ref[0], vc_ref[0]], axis=0)
    low_k = lax.broadcasted_iota(jnp.int32, (n_k, LANES), 1) < HEAD_DIM
    low_q = lax.broadcasted_iota(jnp.int32, (n_q, LANES), 1) < HEAD_DIM
    ones = jnp.ones((n_k, LANES), BF16)

    def both_halves(pair, head_is_low):
        swapped = pltpu.roll(pair, HEAD_DIM, 1)
        return jnp.where(low_k, pair, swapped) if head_is_low else jnp.where(low_k, swapped, pair)

    k_dup, pv_rhs = [], []
    for h in range(KV_HEADS):
        lanes = slice((h // 2) * LANES, (h // 2 + 1) * LANES)
        k_dup.append(both_halves(k_all[:, lanes], h % 2 == 0).astype(BF16))
        v_dup = both_halves(v_all[:, lanes], h % 2 == 0).astype(BF16)
        pv_rhs.append(jnp.concatenate([v_dup, ones], axis=1))

    def one_head(head):
        pair = q[:, (head // 2) * LANES:(head // 2 + 1) * LANES]
        return jnp.where(low_q, pair, 0.0) if head % 2 == 0 else jnp.where(low_q, 0.0, pair)

    for first_head in range(0, Q_HEADS, HEAD_BATCH):
        batch = range(first_head, first_head + HEAD_BATCH)
        kv_heads = range(first_head // GROUP, (first_head + HEAD_BATCH) // GROUP)
        members = lambda h: range(h * GROUP, (h + 1) * GROUP)
        part = lambda tall, i: tall[(i % GROUP) * n_q:(i % GROUP + 1) * n_q]
        qk = {h: _nt(jnp.concatenate([one_head(i) for i in members(h)], axis=0), k_dup[h]) for h in kv_heads}
        s = {i: part(qk[i // GROUP], i) + bias_scr[first + i] for i in batch}
        m = {i: jnp.maximum(jnp.max(s[i], axis=-1, keepdims=True), sinks_ref[i]) for i in batch}
        e = {i: jnp.exp(s[i] - m[i]).astype(BF16) for i in batch}
        pv_tall = {h: _nn(jnp.concatenate([e[i] for i in members(h)], axis=0), pv_rhs[h]) for h in kv_heads}
        pv = {i: part(pv_tall[i // GROUP], i) for i in batch}
        o = {i: pv[i][:, :LANES] / (pv[i][:, LANES:] + jnp.exp(sinks_ref[i] - m[i])) for i in batch}
        for j in range(first_head // 2, (first_head + HEAD_BATCH) // 2):
            lanes = slice(j * LANES, (j + 1) * LANES)
            gate = gb_ref[0, :, lanes]
            yb_ref[0, :, lanes] = (jnp.where(low_q, o[2 * j], o[2 * j + 1])
                                   * (gate * _sigmoid(gate))).astype(BF16)


def _attn(zr3, sinks, hist_k, hist_v, n_q):
    bsz, t_len, _ = zr3.shape
    q_col = (2 * D_A + 2 * D_MODEL) // D_B
    k_col = (2 * D_A + 2 * D_MODEL + D_B) // KV_W
    v_col = k_col + 1
    masked_history = hist_k is None
    if masked_history:
        assert n_q == HIST
        hist_spec = lambda col: pl.BlockSpec((1, HIST, KV_W), lambda b, c: (b, jnp.maximum(c - 1, 0), col))
        hist_specs = [hist_spec(k_col), hist_spec(v_col)]
        hist_k = hist_v = zr3
    else:
        assert t_len == n_q
        hist_specs = [pl.BlockSpec((1, HIST, KV_W), lambda b, c: (b, 0, 0))] * 2
    cur = lambda col: pl.BlockSpec((1, n_q, KV_W), lambda b, c: (b, c, col))
    return pl.pallas_call(
        functools.partial(_attn_kernel, masked_history=masked_history),
        grid=(bsz, t_len // n_q),
        in_specs=[pl.BlockSpec(memory_space=pltpu.SMEM),
                  pl.BlockSpec((1, n_q, D_B), lambda b, c: (b, c, q_col)),
                  hist_specs[0], cur(k_col), hist_specs[1], cur(v_col),
                  pl.BlockSpec((1, n_q, D_B), lambda b, c: (b, c, 1))],
        out_specs=pl.BlockSpec((1, n_q, D_B), lambda b, c: (b, c, 0)),
        out_shape=jax.ShapeDtypeStruct((bsz, t_len, D_B), BF16),
        scratch_shapes=[pltpu.VMEM(((2 if masked_history else 1) * Q_HEADS, n_q, HIST + n_q), F32)],
        compiler_params=pltpu.CompilerParams(dimension_semantics=("arbitrary", "arbitrary")),
        name="attn",
    )(sinks, zr3, hist_k, zr3, hist_v, zr3, zr3)


def _merge_kernel(ya_ref, yb_ref, ma_ref, mb_ref, x_ref, pa_ref, pb_ref, wo_ref, gf_ref, o_ref):
    ua = jnp.dot(ya_ref[...], pa_ref[...], preferred_element_type=F32)
    ub = jnp.dot(yb_ref[...], pb_ref[...], preferred_element_type=F32)
    merged = _sigmoid(ma_ref[...]) * ua + _sigmoid(mb_ref[...]) * ub
    o = x_ref[...] + jnp.dot(merged.astype(BF16), wo_ref[...], preferred_element_type=F32)
    o_ref[...] = o * lax.rsqrt(jnp.mean(o * o, axis=-1, keepdims=True) + RMS_EPS) * gf_ref[...]


def _merge(ya, yb, zr, x2d, pa, pb, wo, gf_row, tm):
    m = x2d.shape[0]
    whole = lambda shape: pl.BlockSpec(shape, lambda i: (0, 0), pipeline_mode=pl.Buffered(1))
    return pl.pallas_call(
        _merge_kernel,
        grid=(m // tm,),
        in_specs=[pl.BlockSpec((tm, D_A), lambda i: (i, 0)),
                  pl.BlockSpec((tm, D_B), lambda i: (i, 0)),
                  pl.BlockSpec((tm, D_MODEL), lambda i: (i, 1)),
                  pl.BlockSpec((tm, D_MODEL), lambda i: (i, 2)),
                  pl.BlockSpec((tm, D_MODEL), lambda i: (i, 0)),
                  whole((D_A, D_MODEL)), whole((D_B, D_MODEL)), whole((D_MODEL, D_MODEL)),
                  whole((1, D_MODEL))],
        out_specs=pl.BlockSpec((tm, D_MODEL), lambda i: (i, 0)),
        out_shape=jax.ShapeDtypeStruct((m, D_MODEL), F32),
        compiler_params=pltpu.CompilerParams(dimension_semantics=("arbitrary",),
                                             vmem_limit_bytes=VMEM_LIMIT),
        name="merge",
    )(ya, yb, zr, zr, x2d, pa, pb, wo, gf_row)


def _layer(x, shift0, wkv0, hist_k, hist_v, wts):
    bsz, t_len, _ = x.shape
    m = bsz * t_len
    x2d = x.reshape(m, D_MODEL)
    zr = _proj(x2d, wts["g_norm"], wts["w_all"], tm=1024, tn=PROJ_TN)
    zr3 = zr.reshape(bsz, t_len, Z_W)
    ya, wkv = _rwkv(zr3, shift0.reshape(bsz, 1, SHIFT_W), wkv0, wts["cparams"], wts["lora_w"], n_sub=1, n_str=4)
    yb = _attn(zr3, wts["sinks"], hist_k, hist_v, n_q=HIST if hist_k is None else t_len)
    y = _merge(ya.reshape(m, D_A), yb.reshape(m, D_B), zr, x2d, wts["p_a"], wts["p_b"], wts["w_o"],
               wts["g_final"], tm=256)
    k_off = 2 * D_A + 2 * D_MODEL + D_B
    tail = zr3[:, -min(t_len, HIST):, :]
    k_new = tail[:, :, k_off:k_off + KV_W]
    v_new = tail[:, :, k_off + KV_W:k_off + 2 * KV_W]
    return y.reshape(bsz, t_len, D_MODEL), wkv, tail[:, -1, REST_W:REST_W + SHIFT_W], k_new, v_new


def _prepare_weights(g_norm, w_in, mu_shift, w0, w_w_up, a0, w_a_up, k_k, k_a, r_k, lnx_w, lnx_b, sinks,
                     p_a, p_b, w_o, g_final):
    o_ga = SHIFT_W
    o_q = o_ga + D_A
    o_k = o_q + D_B
    o_gb = o_k + 2 * KV_W
    o_ma = o_gb + D_B
    w_all = jnp.concatenate([w_in[:, o_ga:o_q], w_in[:, o_gb:o_ma], w_in[:, o_ma:], w_in[:, o_q:o_gb],
                             w_in[:, :SHIFT_W], jnp.zeros((D_MODEL, Z_W - REST_W - SHIFT_W), w_in.dtype)],
                            axis=1).astype(BF16)
    zeros = jnp.zeros((D_A,), F32)
    cparams = jnp.stack([w0, a0, k_k, k_a, r_k.reshape(D_A), lnx_w, lnx_b, zeros,
                         mu_shift[:D_A], mu_shift[D_A:2 * D_A], mu_shift[2 * D_A:3 * D_A],
                         jnp.tile(mu_shift[3 * D_A:], PAIRS), zeros, zeros, zeros, zeros])
    ww = w_w_up.reshape(LORA, PAIRS, LANES).transpose(1, 0, 2)
    wa = w_a_up.reshape(LORA, PAIRS, LANES).transpose(1, 0, 2)
    z = jnp.zeros_like(ww)
    lora_w = jnp.concatenate([jnp.concatenate([ww, z], axis=2), jnp.concatenate([z, wa], axis=2)],
                             axis=1).astype(BF16)
    return dict(g_norm=g_norm.reshape(1, D_MODEL), w_all=w_all, cparams=cparams,
                lora_w=lora_w, sinks=sinks, p_a=p_a.astype(BF16), p_b=p_b.astype(BF16),
                w_o=w_o.astype(BF16), g_final=g_final.reshape(1, D_MODEL))


def kernel(x_prompt, x_sample, state_wkv, state_shift, cache_k, cache_v, g_norm, w_in, mu_shift, w0, w_w_up,
           a0, w_a_up, k_k, k_a, r_k, lnx_w, lnx_b, sinks, p_a, p_b, w_o, g_final):
    assert g_norm.shape[0] == 1, "single-layer stack"
    wts = _prepare_weights(g_norm[0], w_in[0], mu_shift[0], w0[0], w_w_up[0], a0[0], w_a_up[0], k_k[0],
                           k_a[0], r_k[0], lnx_w[0], lnx_b[0], sinks[0], p_a[0], p_b[0], w_o[0], g_final)
    n_p, n_s = x_prompt.shape[0], x_sample.shape[0]
    cache_win = cache_k.shape[2]
    assert cache_win == WINDOW_CHUNKS * CHUNK

    y_p, wkv_p, shift_p, k_p, v_p = _layer(
        x_prompt, jnp.zeros((n_p, SHIFT_W), F32), jnp.zeros((n_p, A_HEADS, HEAD_DIM, HEAD_DIM), F32),
        None, None, wts)
    hist_k = cache_k[0].reshape(n_s, cache_win, KV_W)
    hist_v = cache_v[0].reshape(n_s, cache_win, KV_W)
    y_s, wkv_s, shift_s, k_s, v_s = _layer(x_sample, state_shift[0], state_wkv[0], hist_k, hist_v, wts)

    rows = lambda u, n: u[:, -cache_win:].reshape(n, cache_win, KV_HEADS, HEAD_DIM)[None]
    k_s = jnp.concatenate([hist_k, k_s], axis=1)
    v_s = jnp.concatenate([hist_v, v_s], axis=1)
    return (y_p, y_s,
            wkv_p[None], shift_p[None], rows(k_p, n_p), rows(v_p, n_p),
            wkv_s[None], shift_s[None], rows(k_s, n_s), rows(v_s, n_s))
```

```python
import functools

import jax
import jax.numpy as jnp
from jax import lax
from jax.experimental import pallas as pl
from jax.experimental.pallas import tpu as pltpu

F32 = jnp.float32
BF16 = jnp.bfloat16

D_MODEL = 2048
HEAD_DIM = 64
CHUNK = 64
D_A = 1024
A_HEADS = 16
LORA = 64
SHIFT_W = 3 * D_A + 2 * LORA
D_B = 1024
Q_HEADS = 16
KV_HEADS = 4
GROUP = 4
KV_W = KV_HEADS * HEAD_DIM
WINDOW_CHUNKS = 2
RMS_EPS = 1e-6
LNX_EPS = 64e-5
NEG_INF = -1e30

LANES = 128
PAIRS = D_A // LANES
REST_W = D_A + D_B + 2 * D_MODEL + D_B + 2 * KV_W
PROJ_TN = 1024
Z_W = -(-(REST_W + SHIFT_W) // PROJ_TN) * PROJ_TN
P_BLOCK = 640
P_BLOCKS = SHIFT_W // P_BLOCK
VMEM_LIMIT = 56 * 1024 * 1024


def _nn(a, b):
    return jnp.dot(a.astype(BF16), b.astype(BF16), preferred_element_type=F32)


def _nt(a, b):
    return lax.dot_general(a.astype(BF16), b.astype(BF16), (((1,), (1,)), ((), ())),
                           preferred_element_type=F32)


def _sigmoid(x):
    return 1.0 / (1.0 + jnp.exp(-x))


def _proj_kernel(x_ref, g_ref, w_ref, o_ref, h_scr):
    @pl.when(pl.program_id(1) == 0)
    def _():
        x = x_ref[...]
        h = x * lax.rsqrt(jnp.mean(x * x, axis=-1, keepdims=True) + RMS_EPS) * g_ref[...]
        h_scr[...] = h.astype(BF16)

    o_ref[...] = jnp.dot(h_scr[...], w_ref[...], preferred_element_type=F32)


def _proj(x2d, g_row, w, tm, tn):
    m, n = x2d.shape[0], w.shape[1]
    return pl.pallas_call(
        _proj_kernel,
        grid=(m // tm, n // tn),
        in_specs=[pl.BlockSpec((tm, D_MODEL), lambda i, j: (i, 0)),
                  pl.BlockSpec((1, D_MODEL), lambda i, j: (0, 0)),
                  pl.BlockSpec((D_MODEL, tn), lambda i, j: (0, j))],
        out_specs=pl.BlockSpec((tm, tn), lambda i, j: (i, j)),
        out_shape=jax.ShapeDtypeStruct((m, n), F32),
        scratch_shapes=[pltpu.VMEM((tm, D_MODEL), BF16)],
        compiler_params=pltpu.CompilerParams(dimension_semantics=("arbitrary", "arbitrary"),
                                             vmem_limit_bytes=VMEM_LIMIT),
        name="proj",
    )(x2d, g_row, w)


_W0, _A0, _KK, _KA, _RK, _LNW, _LNB, _MU_R, _MU_K, _MU_V, _MU_WA = 0, 1, 2, 3, 4, 5, 6, 8, 9, 10, 11
_CP_ROWS = 16
RWKV_GROUP = 8
RWKV_SKEW = 3


def _rwkv_stages(p_refs, ga_ref, cp_ref, lw_ref, ya_ref, s_scr, carry_scr, n_sub, items):
    def p_rows(stream, rows, off):
        return p_refs[off // P_BLOCK][stream, rows, off % P_BLOCK:off % P_BLOCK + LANES]

    t_idx = lax.broadcasted_iota(jnp.int32, (CHUNK, LANES), 0)
    lane = lax.broadcasted_iota(jnp.int32, (CHUNK, LANES), 1)
    s_idx = lane & (HEAD_DIM - 1)
    low = lane < HEAD_DIM
    eye = jnp.where(s_idx == t_idx, 1.0, 0.0).astype(F32)
    row4 = lax.broadcasted_iota(jnp.int32, (2 * CHUNK, 2 * LANES), 0)
    col4 = lax.broadcasted_iota(jnp.int32, (2 * CHUNK, 2 * LANES), 1) & (HEAD_DIM - 1)
    causal = ((row4 < CHUNK) & (col4 < row4)) | ((row4 >= CHUNK) & (col4 <= row4 - CHUNK))
    r2 = lax.broadcasted_iota(jnp.int32, (LANES, LANES), 0)
    c2 = lax.broadcasted_iota(jnp.int32, (LANES, LANES), 1)
    same_head = (r2 < HEAD_DIM) == (c2 < HEAD_DIM)
    seg_ones = jnp.where(same_head, 1.0, 0.0).astype(BF16)
    rt = lax.broadcasted_iota(jnp.int32, (CHUNK, CHUNK), 0)
    ct = lax.broadcasted_iota(jnp.int32, (CHUNK, CHUNK), 1)
    tril_ones = jnp.where(ct <= rt, 1.0, 0.0).astype(BF16)

    def stack(q):
        return jnp.concatenate([jnp.where(low, q, 0.0), jnp.where(low, 0.0, q)], axis=0).astype(BF16)

    def seg_sum(xs):
        tall = jnp.concatenate([x.astype(BF16) for x in xs], axis=0)
        out = jnp.dot(tall, seg_ones, preferred_element_type=F32)
        return [out[i * CHUNK:(i + 1) * CHUNK] for i in range(len(xs))]

    wa_off = 3 * D_A
    each = range(len(items))
    stream_of = [stream for stream, _ in items]
    pair_of = [pair for _, pair in items]
    slot_of = [stream_of[i] * PAIRS + pair_of[i] for i in each]
    pair_cols = [slice(pair_of[i] * LANES, (pair_of[i] + 1) * LANES) for i in each]

    for ci in range(n_sub):
        rows = slice(ci * CHUNK, (ci + 1) * CHUNK)
        row = lambda r, i: cp_ref[r:r + 1, pair_cols[i]]

        def shifted(stream, off, mu):
            p = p_rows(stream, rows, off)
            if ci == 0:
                last = carry_scr[stream:stream + 1, off:off + LANES]
            else:
                last = p_rows(stream, slice(ci * CHUNK - 1, ci * CHUNK), off)
            prev = jnp.where(t_idx == 0, last, pltpu.roll(p, 1, 0))
            return p + mu * (prev - p)

        r = [shifted(stream_of[i], pair_of[i] * LANES, row(_MU_R, i)) for i in each]
        k_raw = [shifted(stream_of[i], D_A + pair_of[i] * LANES, row(_MU_K, i)) for i in each]
        v = [shifted(stream_of[i], 2 * D_A + pair_of[i] * LANES, row(_MU_V, i)) for i in each]
        lora_in = {}
        for stream in sorted(set(stream_of)):
            xwa = shifted(stream, wa_off, row(_MU_WA, 0))
            lora_in[stream] = jnp.where(low, jnp.tanh(xwa), xwa).astype(BF16)
        yield
        lora = [_nn(lora_in[stream_of[i]], lw_ref[pair_of[i]]) for i in each]
        yield
        neg = [-(row(_W0, i) + lora[i][:, :LANES]) for i in each]
        softplus = [jnp.maximum(neg[i], 0.0) + jnp.log(1.0 + jnp.exp(-jnp.abs(neg[i]))) for i in each]
        log_decay = [-jnp.exp(-softplus[i] - 0.5) for i in each]
        a_lr = [_sigmoid(row(_A0, i) + lora[i][:, LANES:]) for i in each]
        yield
        kk = [k_raw[i] * row(_KK, i) for i in each]
        kk_sq = seg_sum([kk[i] * kk[i] for i in each])
        yield
        kk = [kk[i] / jnp.maximum(jnp.sqrt(kk_sq[i]), 1e-12) for i in each]
        k = [k_raw[i] * (1.0 + (a_lr[i] - 1.0) * row(_KA, i)) for i in each]
        b_vec = [kk[i] * a_lr[i] for i in each]
        yield
        def prefix_sum(ld):
            hi = ld.astype(BF16)
            res = ld - hi.astype(F32)
            mid = res.astype(BF16)
            lo = (res - mid.astype(F32)).astype(BF16)
            return (jnp.dot(tril_ones, hi, preferred_element_type=F32)
                    + jnp.dot(tril_ones, mid, preferred_element_type=F32)
                    + jnp.dot(tril_ones, lo, preferred_element_type=F32))

        cum = [prefix_sum(log_decay[i]) for i in each]
        yield
        cum_last = [cum[i][CHUNK - 1:CHUNK, :] for i in each]
        g_inv = [jnp.exp(-cum[i]) for i in each]
        g_tail = [jnp.exp(cum_last[i] - cum[i]) for i in each]
        a_t = [-kk[i] * jnp.exp(cum[i] - log_decay[i]) for i in each]
        r_t = [r[i] * jnp.exp(cum[i]) for i in each]
        yield
        stk_b = [stack(b_vec[i] * g_inv[i]) for i in each]
        stk_k = [stack(k[i] * g_inv[i]) for i in each]
        stk_v = [stack(v[i]) for i in each]
        ar = [jnp.concatenate([a_t[i], r_t[i]], axis=0).astype(BF16) for i in each]
        yield
        scores = [jnp.where(causal, _nt(ar[i], jnp.concatenate([stk_b[i], stk_k[i]], axis=0)), 0.0)
                  for i in each]
        a_ab = [scores[i][:CHUNK, :LANES] for i in each]
        a_rb = [scores[i][CHUNK:, :LANES] for i in each]
        a_xk = [scores[i][:, LANES:] for i in each]
        yield
        power = [_nn(a_ab[i], stack(a_ab[i])) for i in each]
        inv = [eye + a_ab[i] for i in each]
        yield
        for _ in range(4):
            both = [_nn(jnp.concatenate([power[i], inv[i]], axis=0), stack(power[i])) for i in each]
            power = [both[i][:CHUNK] for i in each]
            inv = [inv[i] + both[i][CHUNK:] for i in each]
            yield
        inv = [inv[i] + _nn(inv[i], stack(power[i])) for i in each]
        yield
        s_prev = [s_scr[slot_of[i]] for i in each]
        from_state = [_nt(ar[i], s_prev[i]) for i in each]
        from_v = [_nn(a_xk[i], stk_v[i]) for i in each]
        yield
        x = [from_state[i][:CHUNK] + from_v[i][:CHUNK] for i in each]
        u = [_nn(inv[i], stack(x[i])) for i in each]
        yield
        y = [from_state[i][CHUNK:] + from_v[i][CHUNK:] + _nn(a_rb[i], stack(u[i])) for i in each]
        yield
        for i in each:
            uv = jnp.concatenate([u[i], v[i]], axis=0)
            bk = jnp.concatenate([b_vec[i] * g_tail[i], k[i] * g_tail[i]], axis=0)
            s_scr[slot_of[i]] = s_prev[i] * jnp.exp(cum_last[i]) + jnp.where(same_head, _nn(uv.T, bk), 0.0)
        yield
        y_sum = seg_sum(y)
        d = [y[i] - y_sum[i] * (1.0 / HEAD_DIM) for i in each]
        yield
        d_sq = seg_sum([d[i] * d[i] for i in each])
        bonus = seg_sum([r[i] * k[i] * row(_RK, i) for i in each])
        yield
        for i in each:
            var = d_sq[i] * (1.0 / HEAD_DIM)
            yn = d[i] * lax.rsqrt(var + LNX_EPS) * row(_LNW, i) + row(_LNB, i) + bonus[i] * v[i]
            gate = ga_ref[stream_of[i], rows, pair_cols[i]]
            ya_ref[stream_of[i], rows, pair_cols[i]] = (yn * (gate * _sigmoid(gate))).astype(BF16)
        yield


def _staggered(generators, skew):
    pending = list(generators)
    live = []
    rounds = 0
    while pending or live:
        if pending and rounds % skew == 0:
            live.append(pending.pop(0))
        for gen in list(live):
            try:
                next(gen)
            except StopIteration:
                live.remove(gen)
        rounds += 1


def _rwkv_kernel(*refs, n_sub, n_str):
    p_refs = refs[:P_BLOCKS]
    ga_ref, sh_ref, s0_ref, cp_ref, lw_ref, ya_ref, sout_ref, s_scr, carry_scr = refs[P_BLOCKS:]
    c = pl.program_id(1)
    zero64 = jnp.zeros((HEAD_DIM, HEAD_DIM), F32)

    @pl.when(c == 0)
    def _():
        for stream in range(n_str):
            for pair in range(PAIRS):
                s_scr[stream * PAIRS + pair] = jnp.concatenate(
                    [jnp.concatenate([s0_ref[stream, 2 * pair], zero64], axis=1),
                     jnp.concatenate([zero64, s0_ref[stream, 2 * pair + 1]], axis=1)], axis=0)
            carry_scr[stream:stream + 1, :] = sh_ref[stream]

    items = [(stream, pair) for stream in range(n_str) for pair in range(PAIRS)]
    groups = [items[g:g + RWKV_GROUP] for g in range(0, len(items), RWKV_GROUP)]
    _staggered([_rwkv_stages(p_refs, ga_ref, cp_ref, lw_ref, ya_ref, s_scr, carry_scr, n_sub, group)
                for group in groups], RWKV_SKEW)
    for stream in range(n_str):
        for j in range(P_BLOCKS):
            carry_scr[stream:stream + 1, j * P_BLOCK:(j + 1) * P_BLOCK] = (
                p_refs[j][stream, n_sub * CHUNK - 1:n_sub * CHUNK, :])

    @pl.when(c == pl.num_programs(1) - 1)
    def _():
        for i in range(n_str * PAIRS):
            s = s_scr[i]
            sout_ref[i // PAIRS, 2 * (i % PAIRS)] = s[:HEAD_DIM, :HEAD_DIM]
            sout_ref[i // PAIRS, 2 * (i % PAIRS) + 1] = s[HEAD_DIM:, HEAD_DIM:]


def _rwkv(z3, shift0, wkv0, cparams, lora_w, n_sub, n_str):
    bsz, t_len, _ = z3.shape
    tb = n_sub * CHUNK
    state_spec = pl.BlockSpec((n_str, A_HEADS, HEAD_DIM, HEAD_DIM), lambda b, c: (b, 0, 0, 0))
    return pl.pallas_call(
        functools.partial(_rwkv_kernel, n_sub=n_sub, n_str=n_str),
        grid=(bsz // n_str, t_len // tb),
        in_specs=[pl.BlockSpec((n_str, tb, P_BLOCK), lambda b, c, j=j: (b, c, REST_W // P_BLOCK + j))
                  for j in range(P_BLOCKS)]
                 + [pl.BlockSpec((n_str, tb, D_A), lambda b, c: (b, c, 0)),
                  pl.BlockSpec((n_str, 1, SHIFT_W), lambda b, c: (b, 0, 0)),
                  state_spec,
                  pl.BlockSpec((_CP_ROWS, D_A), lambda b, c: (0, 0)),
                  pl.BlockSpec((PAIRS, LANES, 2 * LANES), lambda b, c: (0, 0, 0))],
        out_specs=[pl.BlockSpec((n_str, tb, D_A), lambda b, c: (b, c, 0)), state_spec],
        out_shape=[jax.ShapeDtypeStruct((bsz, t_len, D_A), BF16),
                   jax.ShapeDtypeStruct((bsz, A_HEADS, HEAD_DIM, HEAD_DIM), F32)],
        scratch_shapes=[pltpu.VMEM((n_str * PAIRS, LANES, LANES), F32), pltpu.VMEM((8, SHIFT_W), F32)],
        compiler_params=pltpu.CompilerParams(dimension_semantics=("arbitrary", "arbitrary")),
        name="rwkv",
    )(*([z3] * P_BLOCKS), z3, shift0, wkv0, cparams, lora_w)


_SLOPES = tuple(2.0 ** (-8.0 * (h + 1) / Q_HEADS) for h in range(Q_HEADS))


HIST = WINDOW_CHUNKS * CHUNK
HEAD_BATCH = 8


def _attn_kernel(sinks_ref, q_ref, kh_ref, kc_ref, vh_ref, vc_ref, gb_ref, yb_ref, bias_scr, *, masked_history):
    cb = pl.program_id(1)
    n_q = q_ref.shape[1]
    n_k = HIST + n_q

    @pl.when((pl.program_id(0) == 0) & (cb == 0))
    def _():
        qi = lax.broadcasted_iota(jnp.int32, (n_q, n_k), 0)
        kj = lax.broadcasted_iota(jnp.int32, (n_q, n_k), 1)
        dist = jnp.abs(qi + HIST - kj).astype(F32)
        shift = CHUNK.bit_length() - 1
        back = lax.shift_right_logical(qi, shift) + WINDOW_CHUNKS - lax.shift_right_logical(kj, shift)
        visible = (back >= 0) & (back <= WINDOW_CHUNKS)
        for head in range(Q_HEADS):
            bias_scr[head] = jnp.where(visible, -_SLOPES[head] * dist, NEG_INF)
            if masked_history:
                bias_scr[Q_HEADS + head] = jnp.where(visible & (kj >= HIST), -_SLOPES[head] * dist, NEG_INF)

    first = jnp.where(cb == 0, Q_HEADS, 0) if masked_history else 0
    q = q_ref[0] * (HEAD_DIM ** -0.5)
    k_all = jnp.concatenate([kh_ref[0], kc_ref[0]], axis=0)
    v_all = jnp.concatenate([vh_ref[0], vc_ref[0]], axis=0)
    low_k = lax.broadcasted_iota(jnp.int32, (n_k, LANES), 1) < HEAD_DIM
    low_q = lax.broadcasted_iota(jnp.int32, (n_q, LANES), 1) < HEAD_DIM
    ones = jnp.ones((n_k, LANES), BF16)

    def both_halves(pair, head_is_low):
        swapped = pltpu.roll(pair, HEAD_DIM, 1)
        return jnp.where(low_k, pair, swapped) if head_is_low else jnp.where(low_k, swapped, pair)

    k_dup, pv_rhs = [], []
    for h in range(KV_HEADS):
        lanes = slice((h // 2) * LANES, (h // 2 + 1) * LANES)
        k_dup.append(both_halves(k_all[:, lanes], h % 2 == 0).astype(BF16))
        v_dup = both_halves(v_all[:, lanes], h % 2 == 0).astype(BF16)
        pv_rhs.append(jnp.concatenate([v_dup, ones], axis=1))

    def one_head(head):
        pair = q[:, (head // 2) * LANES:(head // 2 + 1) * LANES]
        return jnp.where(low_q, pair, 0.0) if head % 2 == 0 else jnp.where(low_q, 0.0, pair)

    for first_head in range(0, Q_HEADS, HEAD_BATCH):
        batch = range(first_head, first_head + HEAD_BATCH)
        kv_heads = range(first_head // GROUP, (first_head + HEAD_BATCH) // GROUP)
        members = lambda h: range(h * GROUP, (h + 1) * GROUP)
        part = lambda tall, i: tall[(i % GROUP) * n_q:(i % GROUP + 1) * n_q]
        qk = {h: _nt(jnp.concatenate([one_head(i) for i in members(h)], axis=0), k_dup[h]) for h in kv_heads}
        s = {i: part(qk[i // GROUP], i) + bias_scr[first + i] for i in batch}
        m = {i: jnp.maximum(jnp.max(s[i], axis=-1, keepdims=True), sinks_ref[i]) for i in batch}
        e = {i: jnp.exp(s[i] - m[i]).astype(BF16) for i in batch}
        pv_tall = {h: _nn(jnp.concatenate([e[i] for i in members(h)], axis=0), pv_rhs[h]) for h in kv_heads}
        pv = {i: part(pv_tall[i // GROUP], i) for i in batch}
        o = {i: pv[i][:, :LANES] / (pv[i][:, LANES:] + jnp.exp(sinks_ref[i] - m[i])) for i in batch}
        for j in range(first_head // 2, (first_head + HEAD_BATCH) // 2):
            lanes = slice(j * LANES, (j + 1) * LANES)
            gate = gb_ref[0, :, lanes]
            yb_ref[0, :, lanes] = (jnp.where(low_q, o[2 * j], o[2 * j + 1])
                                   * (gate * _sigmoid(gate))).astype(BF16)


def _attn(zr3, sinks, hist_k, hist_v, n_q):
    bsz, t_len, _ = zr3.shape
    q_col = (2 * D_A + 2 * D_MODEL) // D_B
    k_col = (2 * D_A + 2 * D_MODEL + D_B) // KV_W
    v_col = k_col + 1
    masked_history = hist_k is None
    if masked_history:
        assert n_q == HIST
        hist_spec = lambda col: pl.BlockSpec((1, HIST, KV_W), lambda b, c: (b, jnp.maximum(c - 1, 0), col))
        hist_specs = [hist_spec(k_col), hist_spec(v_col)]
        hist_k = hist_v = zr3
    else:
        assert t_len == n_q
        hist_specs = [pl.BlockSpec((1, HIST, KV_W), lambda b, c: (b, 0, 0))] * 2
    cur = lambda col: pl.BlockSpec((1, n_q, KV_W), lambda b, c: (b, c, col))
    return pl.pallas_call(
        functools.partial(_attn_kernel, masked_history=masked_history),
        grid=(bsz, t_len // n_q),
        in_specs=[pl.BlockSpec(memory_space=pltpu.SMEM),
                  pl.BlockSpec((1, n_q, D_B), lambda b, c: (b, c, q_col)),
                  hist_specs[0], cur(k_col), hist_specs[1], cur(v_col),
                  pl.BlockSpec((1, n_q, D_B), lambda b, c: (b, c, 1))],
        out_specs=pl.BlockSpec((1, n_q, D_B), lambda b, c: (b, c, 0)),
        out_shape=jax.ShapeDtypeStruct((bsz, t_len, D_B), BF16),
        scratch_shapes=[pltpu.VMEM(((2 if masked_history else 1) * Q_HEADS, n_q, HIST + n_q), F32)],
        compiler_params=pltpu.CompilerParams(dimension_semantics=("arbitrary", "arbitrary")),
        name="attn",
    )(sinks, zr3, hist_k, zr3, hist_v, zr3, zr3)


def _merge_kernel(ya_ref, yb_ref, ma_ref, mb_ref, x_ref, pa_ref, pb_ref, wo_ref, gf_ref, o_ref):
    ua = jnp.dot(ya_ref[...], pa_ref[...], preferred_element_type=F32)
    ub = jnp.dot(yb_ref[...], pb_ref[...], preferred_element_type=F32)
    merged = _sigmoid(ma_ref[...]) * ua + _sigmoid(mb_ref[...]) * ub
    o = x_ref[...] + jnp.dot(merged.astype(BF16), wo_ref[...], preferred_element_type=F32)
    o_ref[...] = o * lax.rsqrt(jnp.mean(o * o, axis=-1, keepdims=True) + RMS_EPS) * gf_ref[...]


def _merge(ya, yb, zr, x2d, pa, pb, wo, gf_row, tm):
    m = x2d.shape[0]
    whole = lambda shape: pl.BlockSpec(shape, lambda i: (0, 0), pipeline_mode=pl.Buffered(1))
    return pl.pallas_call(
        _merge_kernel,
        grid=(m // tm,),
        in_specs=[pl.BlockSpec((tm, D_A), lambda i: (i, 0)),
                  pl.BlockSpec((tm, D_B), lambda i: (i, 0)),
                  pl.BlockSpec((tm, D_MODEL), lambda i: (i, 1)),
                  pl.BlockSpec((tm, D_MODEL), lambda i: (i, 2)),
                  pl.BlockSpec((tm, D_MODEL), lambda i: (i, 0)),
                  whole((D_A, D_MODEL)), whole((D_B, D_MODEL)), whole((D_MODEL, D_MODEL)),
                  whole((1, D_MODEL))],
        out_specs=pl.BlockSpec((tm, D_MODEL), lambda i: (i, 0)),
        out_shape=jax.ShapeDtypeStruct((m, D_MODEL), F32),
        compiler_params=pltpu.CompilerParams(dimension_semantics=("arbitrary",),
                                             vmem_limit_bytes=VMEM_LIMIT),
        name="merge",
    )(ya, yb, zr, zr, x2d, pa, pb, wo, gf_row)


def _layer(x, shift0, wkv0, hist_k, hist_v, wts):
    bsz, t_len, _ = x.shape
    m = bsz * t_len
    x2d = x.reshape(m, D_MODEL)
    zr = _proj(x2d, wts["g_norm"], wts["w_all"], tm=1024, tn=PROJ_TN)
    zr3 = zr.reshape(bsz, t_len, Z_W)
    ya, wkv = _rwkv(zr3, shift0.reshape(bsz, 1, SHIFT_W), wkv0, wts["cparams"], wts["lora_w"], n_sub=1, n_str=4)
    yb = _attn(zr3, wts["sinks"], hist_k, hist_v, n_q=HIST if hist_k is None else t_len)
    y = _merge(ya.reshape(m, D_A), yb.reshape(m, D_B), zr, x2d, wts["p_a"], wts["p_b"], wts["w_o"],
               wts["g_final"], tm=256)
    k_off = 2 * D_A + 2 * D_MODEL + D_B
    tail = zr3[:, -min(t_len, HIST):, :]
    k_new = tail[:, :, k_off:k_off + KV_W]
    v_new = tail[:, :, k_off + KV_W:k_off + 2 * KV_W]
    return y.reshape(bsz, t_len, D_MODEL), wkv, tail[:, -1, REST_W:REST_W + SHIFT_W], k_new, v_new


def _prepare_weights(g_norm, w_in, mu_shift, w0, w_w_up, a0, w_a_up, k_k, k_a, r_k, lnx_w, lnx_b, sinks,
                     p_a, p_b, w_o, g_final):
    o_ga = SHIFT_W
    o_q = o_ga + D_A
    o_k = o_q + D_B
    o_gb = o_k + 2 * KV_W
    o_ma = o_gb + D_B
    order = [(o_ga, o_q), (o_gb, o_ma), (o_ma, w_in.shape[1]), (o_q, o_gb), (0, SHIFT_W)]
    blocks = [b for lo, hi in order for b in range(lo // LANES, hi // LANES)]
    blocks += [0] * (Z_W // LANES - len(blocks))
    w_all = jnp.take(w_in.reshape(D_MODEL, -1, LANES), jnp.asarray(blocks, jnp.int32), axis=1)
    w_all = w_all.reshape(D_MODEL, Z_W).astype(BF16)
    zeros = jnp.zeros((D_A,), F32)
    cparams = jnp.stack([w0, a0, k_k, k_a, r_k.reshape(D_A), lnx_w, lnx_b, zeros,
                         mu_shift[:D_A], mu_shift[D_A:2 * D_A], mu_shift[2 * D_A:3 * D_A],
                         jnp.tile(mu_shift[3 * D_A:], PAIRS), zeros, zeros, zeros, zeros])
    ww = w_w_up.reshape(LORA, PAIRS, LANES).transpose(1, 0, 2)
    wa = w_a_up.reshape(LORA, PAIRS, LANES).transpose(1, 0, 2)
    z = jnp.zeros_like(ww)
    lora_w = jnp.concatenate([jnp.concatenate([ww, z], axis=2), jnp.concatenate([z, wa], axis=2)],
                             axis=1).astype(BF16)
    return dict(g_norm=g_norm.reshape(1, D_MODEL), w_all=w_all, cparams=cparams,
                lora_w=lora_w, sinks=sinks, p_a=p_a.astype(BF16), p_b=p_b.astype(BF16),
                w_o=w_o.astype(BF16), g_final=g_final.reshape(1, D_MODEL))


def kernel(x_prompt, x_sample, state_wkv, state_shift, cache_k, cache_v, g_norm, w_in, mu_shift, w0, w_w_up,
           a0, w_a_up, k_k, k_a, r_k, lnx_w, lnx_b, sinks, p_a, p_b, w_o, g_final):
    assert g_norm.shape[0] == 1, "single-layer stack"
    wts = _prepare_weights(g_norm[0], w_in[0], mu_shift[0], w0[0], w_w_up[0], a0[0], w_a_up[0], k_k[0],
                           k_a[0], r_k[0], lnx_w[0], lnx_b[0], sinks[0], p_a[0], p_b[0], w_o[0], g_final)
    n_p, n_s = x_prompt.shape[0], x_sample.shape[0]
    cache_win = cache_k.shape[2]
    assert cache_win == WINDOW_CHUNKS * CHUNK

    y_p, wkv_p, shift_p, k_p, v_p = _layer(
        x_prompt, jnp.zeros((n_p, SHIFT_W), F32), jnp.zeros((n_p, A_HEADS, HEAD_DIM, HEAD_DIM), F32),
        None, None, wts)
    hist_k = cache_k[0].reshape(n_s, cache_win, KV_W)
    hist_v = cache_v[0].reshape(n_s, cache_win, KV_W)
    y_s, wkv_s, shift_s, k_s, v_s = _layer(x_sample, state_shift[0], state_wkv[0], hist_k, hist_v, wts)

    rows = lambda u, n: u[:, -cache_win:].reshape(n, cache_win, KV_HEADS, HEAD_DIM)[None]
    k_s = jnp.concatenate([hist_k, k_s], axis=1)
    v_s = jnp.concatenate([hist_v, v_s], axis=1)
    return (y_p, y_s,
            wkv_p[None], shift_p[None], rows(k_p, n_p), rows(v_p, n_p),
            wkv_s[None], shift_s[None], rows(k_s, n_s), rows(v_s, n_s))
```

```python
import functools

import jax
import jax.numpy as jnp
from jax import lax
from jax.experimental import pallas as pl
from jax.experimental.pallas import tpu as pltpu

F32 = jnp.float32
BF16 = jnp.bfloat16

D_MODEL = 2048
HEAD_DIM = 64
CHUNK = 64
D_A = 1024
A_HEADS = 16
LORA = 64
SHIFT_W = 3 * D_A + 2 * LORA
D_B = 1024
Q_HEADS = 16
KV_HEADS = 4
GROUP = 4
KV_W = KV_HEADS * HEAD_DIM
WINDOW_CHUNKS = 2
RMS_EPS = 1e-6
LNX_EPS = 64e-5
NEG_INF = -1e30

LANES = 128
PAIRS = D_A // LANES
COL_GA = SHIFT_W
COL_Q = COL_GA + D_A
COL_K = COL_Q + D_B
COL_V = COL_K + KV_W
COL_GB = COL_V + KV_W
COL_MA = COL_GB + D_B
COL_MB = COL_MA + D_MODEL
IN_W = COL_MB + D_MODEL
PROJ_TN = 1024
VMEM_LIMIT = 56 * 1024 * 1024


def _nn(a, b):
    return jnp.dot(a.astype(BF16), b.astype(BF16), preferred_element_type=F32)


def _nt(a, b):
    return lax.dot_general(a.astype(BF16), b.astype(BF16), (((1,), (1,)), ((), ())),
                           preferred_element_type=F32)


def _sigmoid(x):
    return 1.0 / (1.0 + jnp.exp(-x))


def _at(shape, start):
    return pl.BlockSpec(tuple(pl.Element(n) for n in shape), start)


def _proj_kernel(x_ref, g_ref, w_ref, o_ref, h_scr):
    @pl.when(pl.program_id(1) == 0)
    def _():
        x = x_ref[...]
        h = x * lax.rsqrt(jnp.mean(x * x, axis=-1, keepdims=True) + RMS_EPS) * g_ref[...]
        h_scr[...] = h.astype(BF16)

    o_ref[...] = jnp.dot(h_scr[...], w_ref[...], preferred_element_type=F32)


def _proj(x2d, g_row, w, tm, tn):
    m, n = x2d.shape[0], w.shape[1]
    return pl.pallas_call(
        _proj_kernel,
        grid=(m // tm, pl.cdiv(n, tn)),
        in_specs=[pl.BlockSpec((tm, D_MODEL), lambda i, j: (i, 0)),
                  pl.BlockSpec((1, D_MODEL), lambda i, j: (0, 0)),
                  pl.BlockSpec((D_MODEL, tn), lambda i, j: (0, j))],
        out_specs=pl.BlockSpec((tm, tn), lambda i, j: (i, j)),
        out_shape=jax.ShapeDtypeStruct((m, n), F32),
        scratch_shapes=[pltpu.VMEM((tm, D_MODEL), BF16)],
        compiler_params=pltpu.CompilerParams(dimension_semantics=("arbitrary", "arbitrary"),
                                             vmem_limit_bytes=VMEM_LIMIT),
        name="proj",
    )(x2d, g_row, w)


_W0, _A0, _KK, _KA, _RK, _LNW, _LNB, _MU_R, _MU_K, _MU_V, _MU_WA = 0, 1, 2, 3, 4, 5, 6, 8, 9, 10, 11
_CP_ROWS = 16
RWKV_GROUP = 8
RWKV_SKEW = 3


def _rwkv_stages(p_ref, ga_ref, cp_ref, lw_ref, ya_ref, s_scr, carry_scr, n_sub, items):
    def p_rows(stream, rows, off):
        return p_ref[stream, rows, off:off + LANES]

    t_idx = lax.broadcasted_iota(jnp.int32, (CHUNK, LANES), 0)
    lane = lax.broadcasted_iota(jnp.int32, (CHUNK, LANES), 1)
    s_idx = lane & (HEAD_DIM - 1)
    low = lane < HEAD_DIM
    eye = jnp.where(s_idx == t_idx, 1.0, 0.0).astype(F32)
    row4 = lax.broadcasted_iota(jnp.int32, (2 * CHUNK, 2 * LANES), 0)
    col4 = lax.broadcasted_iota(jnp.int32, (2 * CHUNK, 2 * LANES), 1) & (HEAD_DIM - 1)
    causal = ((row4 < CHUNK) & (col4 < row4)) | ((row4 >= CHUNK) & (col4 <= row4 - CHUNK))
    r2 = lax.broadcasted_iota(jnp.int32, (LANES, LANES), 0)
    c2 = lax.broadcasted_iota(jnp.int32, (LANES, LANES), 1)
    same_head = (r2 < HEAD_DIM) == (c2 < HEAD_DIM)
    seg_ones = jnp.where(same_head, 1.0, 0.0).astype(BF16)
    rt = lax.broadcasted_iota(jnp.int32, (CHUNK, CHUNK), 0)
    ct = lax.broadcasted_iota(jnp.int32, (CHUNK, CHUNK), 1)
    tril_ones = jnp.where(ct <= rt, 1.0, 0.0).astype(BF16)

    def stack(q):
        return jnp.concatenate([jnp.where(low, q, 0.0), jnp.where(low, 0.0, q)], axis=0).astype(BF16)

    def seg_sum(xs):
        tall = jnp.concatenate([x.astype(BF16) for x in xs], axis=0)
        out = jnp.dot(tall, seg_ones, preferred_element_type=F32)
        return [out[i * CHUNK:(i + 1) * CHUNK] for i in range(len(xs))]

    wa_off = 3 * D_A
    each = range(len(items))
    stream_of = [stream for stream, _ in items]
    pair_of = [pair for _, pair in items]
    slot_of = [stream_of[i] * PAIRS + pair_of[i] for i in each]
    pair_cols = [slice(pair_of[i] * LANES, (pair_of[i] + 1) * LANES) for i in each]

    for ci in range(n_sub):
        rows = slice(ci * CHUNK, (ci + 1) * CHUNK)
        row = lambda r, i: cp_ref[r:r + 1, pair_cols[i]]

        def shifted(stream, off, mu):
            p = p_rows(stream, rows, off)
            if ci == 0:
                last = carry_scr[stream:stream + 1, off:off + LANES]
            else:
                last = p_rows(stream, slice(ci * CHUNK - 1, ci * CHUNK), off)
            prev = jnp.where(t_idx == 0, last, pltpu.roll(p, 1, 0))
            return p + mu * (prev - p)

        r = [shifted(stream_of[i], pair_of[i] * LANES, row(_MU_R, i)) for i in each]
        k_raw = [shifted(stream_of[i], D_A + pair_of[i] * LANES, row(_MU_K, i)) for i in each]
        v = [shifted(stream_of[i], 2 * D_A + pair_of[i] * LANES, row(_MU_V, i)) for i in each]
        lora_in = {}
        for stream in sorted(set(stream_of)):
            xwa = shifted(stream, wa_off, row(_MU_WA, 0))
            lora_in[stream] = jnp.where(low, jnp.tanh(xwa), xwa).astype(BF16)
        yield
        lora = [_nn(lora_in[stream_of[i]], lw_ref[pair_of[i]]) for i in each]
        yield
        neg = [-(row(_W0, i) + lora[i][:, :LANES]) for i in each]
        softplus = [jnp.maximum(neg[i], 0.0) + jnp.log(1.0 + jnp.exp(-jnp.abs(neg[i]))) for i in each]
        log_decay = [-jnp.exp(-softplus[i] - 0.5) for i in each]
        a_lr = [_sigmoid(row(_A0, i) + lora[i][:, LANES:]) for i in each]
        yield
        kk = [k_raw[i] * row(_KK, i) for i in each]
        kk_sq = seg_sum([kk[i] * kk[i] for i in each])
        yield
        kk = [kk[i] / jnp.maximum(jnp.sqrt(kk_sq[i]), 1e-12) for i in each]
        k = [k_raw[i] * (1.0 + (a_lr[i] - 1.0) * row(_KA, i)) for i in each]
        b_vec = [kk[i] * a_lr[i] for i in each]
        yield
        def prefix_sum(ld):
            hi = ld.astype(BF16)
            res = ld - hi.astype(F32)
            mid = res.astype(BF16)
            lo = (res - mid.astype(F32)).astype(BF16)
            return (jnp.dot(tril_ones, hi, preferred_element_type=F32)
                    + jnp.dot(tril_ones, mid, preferred_element_type=F32)
                    + jnp.dot(tril_ones, lo, preferred_element_type=F32))

        cum = [prefix_sum(log_decay[i]) for i in each]
        yield
        cum_last = [cum[i][CHUNK - 1:CHUNK, :] for i in each]
        g_inv = [jnp.exp(-cum[i]) for i in each]
        g_tail = [jnp.exp(cum_last[i] - cum[i]) for i in each]
        a_t = [-kk[i] * jnp.exp(cum[i] - log_decay[i]) for i in each]
        r_t = [r[i] * jnp.exp(cum[i]) for i in each]
        yield
        stk_b = [stack(b_vec[i] * g_inv[i]) for i in each]
        stk_k = [stack(k[i] * g_inv[i]) for i in each]
        stk_v = [stack(v[i]) for i in each]
        ar = [jnp.concatenate([a_t[i], r_t[i]], axis=0).astype(BF16) for i in each]
        yield
        scores = [jnp.where(causal, _nt(ar[i], jnp.concatenate([stk_b[i], stk_k[i]], axis=0)), 0.0)
                  for i in each]
        a_ab = [scores[i][:CHUNK, :LANES] for i in each]
        a_rb = [scores[i][CHUNK:, :LANES] for i in each]
        a_xk = [scores[i][:, LANES:] for i in each]
        yield
        power = [_nn(a_ab[i], stack(a_ab[i])) for i in each]
        inv = [eye + a_ab[i] for i in each]
        yield
        for _ in range(4):
            both = [_nn(jnp.concatenate([power[i], inv[i]], axis=0), stack(power[i])) for i in each]
            power = [both[i][:CHUNK] for i in each]
            inv = [inv[i] + both[i][CHUNK:] for i in each]
            yield
        inv = [inv[i] + _nn(inv[i], stack(power[i])) for i in each]
        yield
        s_prev = [s_scr[slot_of[i]] for i in each]
        from_state = [_nt(ar[i], s_prev[i]) for i in each]
        from_v = [_nn(a_xk[i], stk_v[i]) for i in each]
        yield
        x = [from_state[i][:CHUNK] + from_v[i][:CHUNK] for i in each]
        u = [_nn(inv[i], stack(x[i])) for i in each]
        yield
        y = [from_state[i][CHUNK:] + from_v[i][CHUNK:] + _nn(a_rb[i], stack(u[i])) for i in each]
        yield
        for i in each:
            uv = jnp.concatenate([u[i], v[i]], axis=0)
            bk = jnp.concatenate([b_vec[i] * g_tail[i], k[i] * g_tail[i]], axis=0)
            s_scr[slot_of[i]] = s_prev[i] * jnp.exp(cum_last[i]) + jnp.where(same_head, _nn(uv.T, bk), 0.0)
        yield
        y_sum = seg_sum(y)
        d = [y[i] - y_sum[i] * (1.0 / HEAD_DIM) for i in each]
        yield
        d_sq = seg_sum([d[i] * d[i] for i in each])
        bonus = seg_sum([r[i] * k[i] * row(_RK, i) for i in each])
        yield
        for i in each:
            var = d_sq[i] * (1.0 / HEAD_DIM)
            yn = d[i] * lax.rsqrt(var + LNX_EPS) * row(_LNW, i) + row(_LNB, i) + bonus[i] * v[i]
            gate = ga_ref[stream_of[i], rows, pair_cols[i]]
            ya_ref[stream_of[i], rows, pair_cols[i]] = (yn * (gate * _sigmoid(gate))).astype(BF16)
        yield


def _staggered(generators, skew):
    pending = list(generators)
    live = []
    rounds = 0
    while pending or live:
        if pending and rounds % skew == 0:
            live.append(pending.pop(0))
        for gen in list(live):
            try:
                next(gen)
            except StopIteration:
                live.remove(gen)
        rounds += 1


def _rwkv_kernel(*refs, n_sub, n_str):
    p_ref, ga_ref, sh_ref, s0_ref, cp_ref, lw_ref, ya_ref, sout_ref, s_scr, carry_scr = refs
    c = pl.program_id(1)
    zero64 = jnp.zeros((HEAD_DIM, HEAD_DIM), F32)

    @pl.when(c == 0)
    def _():
        for stream in range(n_str):
            for pair in range(PAIRS):
                s_scr[stream * PAIRS + pair] = jnp.concatenate(
                    [jnp.concatenate([s0_ref[stream, 2 * pair], zero64], axis=1),
                     jnp.concatenate([zero64, s0_ref[stream, 2 * pair + 1]], axis=1)], axis=0)
            carry_scr[stream:stream + 1, :] = sh_ref[stream]

    items = [(stream, pair) for stream in range(n_str) for pair in range(PAIRS)]
    groups = [items[g:g + RWKV_GROUP] for g in range(0, len(items), RWKV_GROUP)]
    _staggered([_rwkv_stages(p_ref, ga_ref, cp_ref, lw_ref, ya_ref, s_scr, carry_scr, n_sub, group)
                for group in groups], RWKV_SKEW)
    for stream in range(n_str):
        carry_scr[stream:stream + 1, :] = p_ref[stream, n_sub * CHUNK - 1:n_sub * CHUNK, :]

    @pl.when(c == pl.num_programs(1) - 1)
    def _():
        for i in range(n_str * PAIRS):
            s = s_scr[i]
            sout_ref[i // PAIRS, 2 * (i % PAIRS)] = s[:HEAD_DIM, :HEAD_DIM]
            sout_ref[i // PAIRS, 2 * (i % PAIRS) + 1] = s[HEAD_DIM:, HEAD_DIM:]


def _rwkv(z3, shift0, wkv0, cparams, lora_w, n_sub, n_str):
    bsz, t_len, _ = z3.shape
    tb = n_sub * CHUNK
    state_spec = pl.BlockSpec((n_str, A_HEADS, HEAD_DIM, HEAD_DIM), lambda b, c: (b, 0, 0, 0))
    return pl.pallas_call(
        functools.partial(_rwkv_kernel, n_sub=n_sub, n_str=n_str),
        grid=(bsz // n_str, t_len // tb),
        in_specs=[pl.BlockSpec((n_str, tb, SHIFT_W), lambda b, c: (b, c, 0)),
                  _at((n_str, tb, D_A), lambda b, c: (b * n_str, c * tb, COL_GA)),
                  pl.BlockSpec((n_str, 1, SHIFT_W), lambda b, c: (b, 0, 0)),
                  state_spec,
                  pl.BlockSpec((_CP_ROWS, D_A), lambda b, c: (0, 0)),
                  pl.BlockSpec((PAIRS, LANES, 2 * LANES), lambda b, c: (0, 0, 0))],
        out_specs=[pl.BlockSpec((n_str, tb, D_A), lambda b, c: (b, c, 0)), state_spec],
        out_shape=[jax.ShapeDtypeStruct((bsz, t_len, D_A), BF16),
                   jax.ShapeDtypeStruct((bsz, A_HEADS, HEAD_DIM, HEAD_DIM), F32)],
        scratch_shapes=[pltpu.VMEM((n_str * PAIRS, LANES, LANES), F32), pltpu.VMEM((8, SHIFT_W), F32)],
        compiler_params=pltpu.CompilerParams(dimension_semantics=("arbitrary", "arbitrary")),
        name="rwkv",
    )(z3, z3, shift0, wkv0, cparams, lora_w)


_SLOPES = tuple(2.0 ** (-8.0 * (h + 1) / Q_HEADS) for h in range(Q_HEADS))


HIST = WINDOW_CHUNKS * CHUNK
HEAD_BATCH = 8


def _attn_kernel(sinks_ref, q_ref, kh_ref, kc_ref, vh_ref, vc_ref, gb_ref, yb_ref, bias_scr, *, masked_history):
    cb = pl.program_id(1)
    n_q = q_ref.shape[1]
    n_k = HIST + n_q

    @pl.when((pl.program_id(0) == 0) & (cb == 0))
    def _():
        qi = lax.broadcasted_iota(jnp.int32, (n_q, n_k), 0)
        kj = lax.broadcasted_iota(jnp.int32, (n_q, n_k), 1)
        dist = jnp.abs(qi + HIST - kj).astype(F32)
        shift = CHUNK.bit_length() - 1
        back = lax.shift_right_logical(qi, shift) + WINDOW_CHUNKS - lax.shift_right_logical(kj, shift)
        visible = (back >= 0) & (back <= WINDOW_CHUNKS)
        for head in range(Q_HEADS):
            bias_scr[head] = jnp.where(visible, -_SLOPES[head] * dist, NEG_INF)
            if masked_history:
                bias_scr[Q_HEADS + head] = jnp.where(visible & (kj >= HIST), -_SLOPES[head] * dist, NEG_INF)

    first = jnp.where(cb == 0, Q_HEADS, 0) if masked_history else 0
    q = q_ref[0] * (HEAD_DIM ** -0.5)
    k_all = jnp.concatenate([kh_ref[0], kc_ref[0]], axis=0)
    v_all = jnp.concatenate([vh_ref[0], vc_ref[0]], axis=0)
    low_k = lax.broadcasted_iota(jnp.int32, (n_k, LANES), 1) < HEAD_DIM
    low_q = lax.broadcasted_iota(jnp.int32, (n_q, LANES), 1) < HEAD_DIM
    ones = jnp.ones((n_k, LANES), BF16)

    def both_halves(pair, head_is_low):
        swapped = pltpu.roll(pair, HEAD_DIM, 1)
        return jnp.where(low_k, pair, swapped) if head_is_low else jnp.where(low_k, swapped, pair)

    k_dup, pv_rhs = [], []
    for h in range(KV_HEADS):
        lanes = slice((h // 2) * LANES, (h // 2 + 1) * LANES)
        k_dup.append(both_halves(k_all[:, lanes], h % 2 == 0).astype(BF16))
        v_dup = both_halves(v_all[:, lanes], h % 2 == 0).astype(BF16)
        pv_rhs.append(jnp.concatenate([v_dup, ones], axis=1))

    def one_head(head):
        pair = q[:, (head // 2) * LANES:(head // 2 + 1) * LANES]
        return jnp.where(low_q, pair, 0.0) if head % 2 == 0 else jnp.where(low_q, 0.0, pair)

    for first_head in range(0, Q_HEADS, HEAD_BATCH):
        batch = range(first_head, first_head + HEAD_BATCH)
        kv_heads = range(first_head // GROUP, (first_head + HEAD_BATCH) // GROUP)
        members = lambda h: range(h * GROUP, (h + 1) * GROUP)
        part = lambda tall, i: tall[(i % GROUP) * n_q:(i % GROUP + 1) * n_q]
        qk = {h: _nt(jnp.concatenate([one_head(i) for i in members(h)], axis=0), k_dup[h]) for h in kv_heads}
        s = {i: part(qk[i // GROUP], i) + bias_scr[first + i] for i in batch}
        m = {i: jnp.maximum(jnp.max(s[i], axis=-1, keepdims=True), sinks_ref[i]) for i in batch}
        e = {i: jnp.exp(s[i] - m[i]).astype(BF16) for i in batch}
        pv_tall = {h: _nn(jnp.concatenate([e[i] for i in members(h)], axis=0), pv_rhs[h]) for h in kv_heads}
        pv = {i: part(pv_tall[i // GROUP], i) for i in batch}
        o = {i: pv[i][:, :LANES] / (pv[i][:, LANES:] + jnp.exp(sinks_ref[i] - m[i])) for i in batch}
        for j in range(first_head // 2, (first_head + HEAD_BATCH) // 2):
            lanes = slice(j * LANES, (j + 1) * LANES)
            gate = gb_ref[0, :, lanes]
            yb_ref[0, :, lanes] = (jnp.where(low_q, o[2 * j], o[2 * j + 1])
                                   * (gate * _sigmoid(gate))).astype(BF16)


def _attn(zr3, sinks, hist_k, hist_v, n_q):
    bsz, t_len, _ = zr3.shape
    masked_history = hist_k is None
    if masked_history:
        assert n_q == HIST
        hist_spec = lambda col: _at((1, HIST, KV_W), lambda b, c: (b, jnp.maximum(c - 1, 0) * HIST, col))
        hist_specs = [hist_spec(COL_K), hist_spec(COL_V)]
        hist_k = hist_v = zr3
    else:
        assert t_len == n_q
        hist_specs = [pl.BlockSpec((1, HIST, KV_W), lambda b, c: (b, 0, 0))] * 2
    cols = lambda width, col: _at((1, n_q, width), lambda b, c: (b, c * n_q, col))
    return pl.pallas_call(
        functools.partial(_attn_kernel, masked_history=masked_history),
        grid=(bsz, t_len // n_q),
        in_specs=[pl.BlockSpec(memory_space=pltpu.SMEM),
                  cols(D_B, COL_Q),
                  hist_specs[0], cols(KV_W, COL_K), hist_specs[1], cols(KV_W, COL_V),
                  cols(D_B, COL_GB)],
        out_specs=pl.BlockSpec((1, n_q, D_B), lambda b, c: (b, c, 0)),
        out_shape=jax.ShapeDtypeStruct((bsz, t_len, D_B), BF16),
        scratch_shapes=[pltpu.VMEM(((2 if masked_history else 1) * Q_HEADS, n_q, HIST + n_q), F32)],
        compiler_params=pltpu.CompilerParams(dimension_semantics=("arbitrary", "arbitrary")),
        name="attn",
    )(sinks, zr3, hist_k, zr3, hist_v, zr3, zr3)


def _merge_kernel(ya_ref, yb_ref, ma_ref, mb_ref, x_ref, pa_ref, pb_ref, wo_ref, gf_ref, o_ref):
    ua = jnp.dot(ya_ref[...], pa_ref[...], preferred_element_type=F32)
    ub = jnp.dot(yb_ref[...], pb_ref[...], preferred_element_type=F32)
    merged = _sigmoid(ma_ref[...]) * ua + _sigmoid(mb_ref[...]) * ub
    o = x_ref[...] + jnp.dot(merged.astype(BF16), wo_ref[...], preferred_element_type=F32)
    o_ref[...] = o * lax.rsqrt(jnp.mean(o * o, axis=-1, keepdims=True) + RMS_EPS) * gf_ref[...]


def _merge(ya, yb, zr, x2d, pa, pb, wo, gf_row, tm):
    m = x2d.shape[0]
    whole = lambda shape: pl.BlockSpec(shape, lambda i: (0, 0), pipeline_mode=pl.Buffered(1))
    return pl.pallas_call(
        _merge_kernel,
        grid=(m // tm,),
        in_specs=[pl.BlockSpec((tm, D_A), lambda i: (i, 0)),
                  pl.BlockSpec((tm, D_B), lambda i: (i, 0)),
                  _at((tm, D_MODEL), lambda i: (i * tm, COL_MA)),
                  _at((tm, D_MODEL), lambda i: (i * tm, COL_MB)),
                  pl.BlockSpec((tm, D_MODEL), lambda i: (i, 0)),
                  whole((D_A, D_MODEL)), whole((D_B, D_MODEL)), whole((D_MODEL, D_MODEL)),
                  whole((1, D_MODEL))],
        out_specs=pl.BlockSpec((tm, D_MODEL), lambda i: (i, 0)),
        out_shape=jax.ShapeDtypeStruct((m, D_MODEL), F32),
        compiler_params=pltpu.CompilerParams(dimension_semantics=("arbitrary",),
                                             vmem_limit_bytes=VMEM_LIMIT),
        name="merge",
    )(ya, yb, zr, zr, x2d, pa, pb, wo, gf_row)


def _layer(x, shift0, wkv0, hist_k, hist_v, wts):
    bsz, t_len, _ = x.shape
    m = bsz * t_len
    x2d = x.reshape(m, D_MODEL)
    zr = _proj(x2d, wts["g_norm"], wts["w_all"], tm=1024, tn=PROJ_TN)
    zr3 = zr.reshape(bsz, t_len, IN_W)
    ya, wkv = _rwkv(zr3, shift0.reshape(bsz, 1, SHIFT_W), wkv0, wts["cparams"], wts["lora_w"], n_sub=1, n_str=4)
    yb = _attn(zr3, wts["sinks"], hist_k, hist_v, n_q=HIST if hist_k is None else t_len)
    y = _merge(ya.reshape(m, D_A), yb.reshape(m, D_B), zr, x2d, wts["p_a"], wts["p_b"], wts["w_o"],
               wts["g_final"], tm=256)
    tail = zr3[:, -min(t_len, HIST):, :]
    k_new = tail[:, :, COL_K:COL_K + KV_W]
    v_new = tail[:, :, COL_V:COL_V + KV_W]
    return y.reshape(bsz, t_len, D_MODEL), wkv, tail[:, -1, :SHIFT_W], k_new, v_new


def _prepare_weights(g_norm, w_in, mu_shift, w0, w_w_up, a0, w_a_up, k_k, k_a, r_k, lnx_w, lnx_b, sinks,
                     p_a, p_b, w_o, g_final):
    assert w_in.shape[1] == IN_W
    w_all = w_in.astype(BF16)
    zeros = jnp.zeros((D_A,), F32)
    cparams = jnp.stack([w0, a0, k_k, k_a, r_k.reshape(D_A), lnx_w, lnx_b, zeros,
                         mu_shift[:D_A], mu_shift[D_A:2 * D_A], mu_shift[2 * D_A:3 * D_A],
                         jnp.tile(mu_shift[3 * D_A:], PAIRS), zeros, zeros, zeros, zeros])
    ww = w_w_up.reshape(LORA, PAIRS, LANES).transpose(1, 0, 2)
    wa = w_a_up.reshape(LORA, PAIRS, LANES).transpose(1, 0, 2)
    z = jnp.zeros_like(ww)
    lora_w = jnp.concatenate([jnp.concatenate([ww, z], axis=2), jnp.concatenate([z, wa], axis=2)],
                             axis=1).astype(BF16)
    return dict(g_norm=g_norm.reshape(1, D_MODEL), w_all=w_all, cparams=cparams,
                lora_w=lora_w, sinks=sinks, p_a=p_a.astype(BF16), p_b=p_b.astype(BF16),
                w_o=w_o.astype(BF16), g_final=g_final.reshape(1, D_MODEL))


def kernel(x_prompt, x_sample, state_wkv, state_shift, cache_k, cache_v, g_norm, w_in, mu_shift, w0, w_w_up,
           a0, w_a_up, k_k, k_a, r_k, lnx_w, lnx_b, sinks, p_a, p_b, w_o, g_final):
    assert g_norm.shape[0] == 1, "single-layer stack"
    wts = _prepare_weights(g_norm[0], w_in[0], mu_shift[0], w0[0], w_w_up[0], a0[0], w_a_up[0], k_k[0],
                           k_a[0], r_k[0], lnx_w[0], lnx_b[0], sinks[0], p_a[0], p_b[0], w_o[0], g_final)
    n_p, n_s = x_prompt.shape[0], x_sample.shape[0]
    cache_win = cache_k.shape[2]
    assert cache_win == WINDOW_CHUNKS * CHUNK

    y_p, wkv_p, shift_p, k_p, v_p = _layer(
        x_prompt, jnp.zeros((n_p, SHIFT_W), F32), jnp.zeros((n_p, A_HEADS, HEAD_DIM, HEAD_DIM), F32),
        None, None, wts)
    hist_k = cache_k[0].reshape(n_s, cache_win, KV_W)
    hist_v = cache_v[0].reshape(n_s, cache_win, KV_W)
    y_s, wkv_s, shift_s, k_s, v_s = _layer(x_sample, state_shift[0], state_wkv[0], hist_k, hist_v, wts)

    rows = lambda u, n: u[:, -cache_win:].reshape(n, cache_win, KV_HEADS, HEAD_DIM)[None]
    k_s = jnp.concatenate([hist_k, k_s], axis=1)
    v_s = jnp.concatenate([hist_v, v_s], axis=1)
    return (y_p, y_s,
            wkv_p[None], shift_p[None], rows(k_p, n_p), rows(v_p, n_p),
            wkv_s[None], shift_s[None], rows(k_s, n_s), rows(v_s, n_s))
```

```python
import functools

import jax
import jax.numpy as jnp
from jax import lax
from jax.experimental import pallas as pl
from jax.experimental.pallas import tpu as pltpu

F32 = jnp.float32
BF16 = jnp.bfloat16

D_MODEL = 2048
HEAD_DIM = 64
CHUNK = 64
D_A = 1024
A_HEADS = 16
LORA = 64
SHIFT_W = 3 * D_A + 2 * LORA
D_B = 1024
Q_HEADS = 16
KV_HEADS = 4
GROUP = 4
KV_W = KV_HEADS * HEAD_DIM
WINDOW_CHUNKS = 2
RMS_EPS = 1e-6
LNX_EPS = 64e-5
NEG_INF = -1e30

LANES = 128
MXU_N = 256
PAIRS = D_A // LANES
COL_GA = SHIFT_W
COL_Q = COL_GA + D_A
COL_K = COL_Q + D_B
COL_V = COL_K + KV_W
COL_GB = COL_V + KV_W
COL_MA = COL_GB + D_B
COL_MB = COL_MA + D_MODEL
IN_W = COL_MB + D_MODEL
PROJ_TN = 1024
VMEM_LIMIT = 56 * 1024 * 1024


def _nn(a, b):
    return jnp.dot(a.astype(BF16), b.astype(BF16), preferred_element_type=F32)


def _nt(a, b):
    return lax.dot_general(a.astype(BF16), b.astype(BF16), (((1,), (1,)), ((), ())),
                           preferred_element_type=F32)


def _sigmoid(x):
    return 1.0 / (1.0 + jnp.exp(-x))


def _at(shape, start):
    return pl.BlockSpec(tuple(pl.Element(n) for n in shape), start)


def _proj_kernel(x_ref, g_ref, w_ref, o_ref, h_scr, *, last_cols):
    j = pl.program_id(1)
    last = pl.num_programs(1) - 1

    @pl.when(j == 0)
    def _():
        x = x_ref[...]
        h = x * lax.rsqrt(jnp.mean(x * x, axis=-1, keepdims=True) + RMS_EPS) * g_ref[...]
        h_scr[...] = h.astype(BF16)

    @pl.when(j < last)
    def _():
        o_ref[...] = jnp.dot(h_scr[...], w_ref[...], preferred_element_type=F32)

    @pl.when(j == last)
    def _():
        o_ref[:, :last_cols] = jnp.dot(h_scr[...], w_ref[:, :last_cols], preferred_element_type=F32)


def _proj(x2d, g_row, w, tm, tn):
    m, n = x2d.shape[0], w.shape[1]
    last_cols = -(-(n - (pl.cdiv(n, tn) - 1) * tn) // MXU_N) * MXU_N
    return pl.pallas_call(
        functools.partial(_proj_kernel, last_cols=last_cols),
        grid=(m // tm, pl.cdiv(n, tn)),
        in_specs=[pl.BlockSpec((tm, D_MODEL), lambda i, j: (i, 0)),
                  pl.BlockSpec((1, D_MODEL), lambda i, j: (0, 0)),
                  pl.BlockSpec((D_MODEL, tn), lambda i, j: (0, j))],
        out_specs=pl.BlockSpec((tm, tn), lambda i, j: (i, j)),
        out_shape=jax.ShapeDtypeStruct((m, n), F32),
        scratch_shapes=[pltpu.VMEM((tm, D_MODEL), BF16)],
        compiler_params=pltpu.CompilerParams(dimension_semantics=("arbitrary", "arbitrary"),
                                             vmem_limit_bytes=VMEM_LIMIT),
        name="proj",
    )(x2d, g_row, w)


_W0, _A0, _KK, _KA, _RK, _LNW, _LNB, _MU_R, _MU_K, _MU_V, _MU_WA = 0, 1, 2, 3, 4, 5, 6, 8, 9, 10, 11
_CP_ROWS = 16
RWKV_GROUP = 8
RWKV_SKEW = 3


def _rwkv_stages(p_ref, ga_ref, cp_ref, lw_ref, ya_ref, s_scr, carry_scr, n_sub, items):
    def p_rows(stream, rows, off):
        return p_ref[stream, rows, off:off + LANES]

    t_idx = lax.broadcasted_iota(jnp.int32, (CHUNK, LANES), 0)
    lane = lax.broadcasted_iota(jnp.int32, (CHUNK, LANES), 1)
    s_idx = lane & (HEAD_DIM - 1)
    low = lane < HEAD_DIM
    eye = jnp.where(s_idx == t_idx, 1.0, 0.0).astype(F32)
    row4 = lax.broadcasted_iota(jnp.int32, (2 * CHUNK, 2 * LANES), 0)
    col4 = lax.broadcasted_iota(jnp.int32, (2 * CHUNK, 2 * LANES), 1) & (HEAD_DIM - 1)
    causal = ((row4 < CHUNK) & (col4 < row4)) | ((row4 >= CHUNK) & (col4 <= row4 - CHUNK))
    r2 = lax.broadcasted_iota(jnp.int32, (LANES, LANES), 0)
    c2 = lax.broadcasted_iota(jnp.int32, (LANES, LANES), 1)
    same_head = (r2 < HEAD_DIM) == (c2 < HEAD_DIM)
    seg_ones = jnp.where(same_head, 1.0, 0.0).astype(BF16)
    rt = lax.broadcasted_iota(jnp.int32, (CHUNK, CHUNK), 0)
    ct = lax.broadcasted_iota(jnp.int32, (CHUNK, CHUNK), 1)
    tril_ones = jnp.where(ct <= rt, 1.0, 0.0).astype(BF16)

    def stack(q):
        return jnp.concatenate([jnp.where(low, q, 0.0), jnp.where(low, 0.0, q)], axis=0).astype(BF16)

    def seg_sum(xs):
        tall = jnp.concatenate([x.astype(BF16) for x in xs], axis=0)
        out = jnp.dot(tall, seg_ones, preferred_element_type=F32)
        return [out[i * CHUNK:(i + 1) * CHUNK] for i in range(len(xs))]

    wa_off = 3 * D_A
    each = range(len(items))
    stream_of = [stream for stream, _ in items]
    pair_of = [pair for _, pair in items]
    slot_of = [stream_of[i] * PAIRS + pair_of[i] for i in each]
    pair_cols = [slice(pair_of[i] * LANES, (pair_of[i] + 1) * LANES) for i in each]

    for ci in range(n_sub):
        rows = slice(ci * CHUNK, (ci + 1) * CHUNK)
        row = lambda r, i: cp_ref[r:r + 1, pair_cols[i]]

        def shifted(stream, off, mu):
            p = p_rows(stream, rows, off)
            if ci == 0:
                last = carry_scr[stream:stream + 1, off:off + LANES]
            else:
                last = p_rows(stream, slice(ci * CHUNK - 1, ci * CHUNK), off)
            prev = jnp.where(t_idx == 0, last, pltpu.roll(p, 1, 0))
            return p + mu * (prev - p)

        r = [shifted(stream_of[i], pair_of[i] * LANES, row(_MU_R, i)) for i in each]
        k_raw = [shifted(stream_of[i], D_A + pair_of[i] * LANES, row(_MU_K, i)) for i in each]
        v = [shifted(stream_of[i], 2 * D_A + pair_of[i] * LANES, row(_MU_V, i)) for i in each]
        lora_in = {}
        for stream in sorted(set(stream_of)):
            xwa = shifted(stream, wa_off, row(_MU_WA, 0))
            lora_in[stream] = jnp.where(low, jnp.tanh(xwa), xwa).astype(BF16)
        yield
        lora = [_nn(lora_in[stream_of[i]], lw_ref[pair_of[i]]) for i in each]
        yield
        neg = [-(row(_W0, i) + lora[i][:, :LANES]) for i in each]
        softplus = [jnp.maximum(neg[i], 0.0) + jnp.log(1.0 + jnp.exp(-jnp.abs(neg[i]))) for i in each]
        log_decay = [-jnp.exp(-softplus[i] - 0.5) for i in each]
        a_lr = [_sigmoid(row(_A0, i) + lora[i][:, LANES:]) for i in each]
        yield
        kk = [k_raw[i] * row(_KK, i) for i in each]
        kk_sq = seg_sum([kk[i] * kk[i] for i in each])
        yield
        kk = [kk[i] / jnp.maximum(jnp.sqrt(kk_sq[i]), 1e-12) for i in each]
        k = [k_raw[i] * (1.0 + (a_lr[i] - 1.0) * row(_KA, i)) for i in each]
        b_vec = [kk[i] * a_lr[i] for i in each]
        yield
        def prefix_sum(ld):
            hi = ld.astype(BF16)
            res = ld - hi.astype(F32)
            mid = res.astype(BF16)
            lo = (res - mid.astype(F32)).astype(BF16)
            return (jnp.dot(tril_ones, hi, preferred_element_type=F32)
                    + jnp.dot(tril_ones, mid, preferred_element_type=F32)
                    + jnp.dot(tril_ones, lo, preferred_element_type=F32))

        cum = [prefix_sum(log_decay[i]) for i in each]
        yield
        cum_last = [cum[i][CHUNK - 1:CHUNK, :] for i in each]
        g_inv = [jnp.exp(-cum[i]) for i in each]
        g_tail = [jnp.exp(cum_last[i] - cum[i]) for i in each]
        a_t = [-kk[i] * jnp.exp(cum[i] - log_decay[i]) for i in each]
        r_t = [r[i] * jnp.exp(cum[i]) for i in each]
        yield
        stk_b = [stack(b_vec[i] * g_inv[i]) for i in each]
        stk_k = [stack(k[i] * g_inv[i]) for i in each]
        stk_v = [stack(v[i]) for i in each]
        ar = [jnp.concatenate([a_t[i], r_t[i]], axis=0).astype(BF16) for i in each]
        yield
        scores = [jnp.where(causal, _nt(ar[i], jnp.concatenate([stk_b[i], stk_k[i]], axis=0)), 0.0)
                  for i in each]
        a_ab = [scores[i][:CHUNK, :LANES] for i in each]
        a_rb = [scores[i][CHUNK:, :LANES] for i in each]
        a_xk = [scores[i][:, LANES:] for i in each]
        yield
        power = [_nn(a_ab[i], stack(a_ab[i])) for i in each]
        inv = [eye + a_ab[i] for i in each]
        yield
        for _ in range(4):
            both = [_nn(jnp.concatenate([power[i], inv[i]], axis=0), stack(power[i])) for i in each]
            power = [both[i][:CHUNK] for i in each]
            inv = [inv[i] + both[i][CHUNK:] for i in each]
            yield
        inv = [inv[i] + _nn(inv[i], stack(power[i])) for i in each]
        yield
        s_prev = [s_scr[slot_of[i]] for i in each]
        from_state = [_nt(ar[i], s_prev[i]) for i in each]
        from_v = [_nn(a_xk[i], stk_v[i]) for i in each]
        yield
        x = [from_state[i][:CHUNK] + from_v[i][:CHUNK] for i in each]
        u = [_nn(inv[i], stack(x[i])) for i in each]
        yield
        y = [from_state[i][CHUNK:] + from_v[i][CHUNK:] + _nn(a_rb[i], stack(u[i])) for i in each]
        yield
        for i in each:
            uv = jnp.concatenate([u[i], v[i]], axis=0)
            bk = jnp.concatenate([b_vec[i] * g_tail[i], k[i] * g_tail[i]], axis=0)
            s_scr[slot_of[i]] = s_prev[i] * jnp.exp(cum_last[i]) + jnp.where(same_head, _nn(uv.T, bk), 0.0)
        yield
        y_sum = seg_sum(y)
        d = [y[i] - y_sum[i] * (1.0 / HEAD_DIM) for i in each]
        yield
        d_sq = seg_sum([d[i] * d[i] for i in each])
        bonus = seg_sum([r[i] * k[i] * row(_RK, i) for i in each])
        yield
        for i in each:
            var = d_sq[i] * (1.0 / HEAD_DIM)
            yn = d[i] * lax.rsqrt(var + LNX_EPS) * row(_LNW, i) + row(_LNB, i) + bonus[i] * v[i]
            gate = ga_ref[stream_of[i], rows, pair_cols[i]]
            ya_ref[stream_of[i], rows, pair_cols[i]] = (yn * (gate * _sigmoid(gate))).astype(BF16)
        yield


def _staggered(generators, skew):
    pending = list(generators)
    live = []
    rounds = 0
    while pending or live:
        if pending and rounds % skew == 0:
            live.append(pending.pop(0))
        for gen in list(live):
            try:
                next(gen)
            except StopIteration:
                live.remove(gen)
        rounds += 1


def _rwkv_kernel(*refs, n_sub, n_str):
    p_ref, ga_ref, sh_ref, s0_ref, cp_ref, lw_ref, ya_ref, sout_ref, s_scr, carry_scr = refs
    c = pl.program_id(1)
    zero64 = jnp.zeros((HEAD_DIM, HEAD_DIM), F32)

    @pl.when(c == 0)
    def _():
        for stream in range(n_str):
            for pair in range(PAIRS):
                s_scr[stream * PAIRS + pair] = jnp.concatenate(
                    [jnp.concatenate([s0_ref[stream, 2 * pair], zero64], axis=1),
                     jnp.concatenate([zero64, s0_ref[stream, 2 * pair + 1]], axis=1)], axis=0)
            carry_scr[stream:stream + 1, :] = sh_ref[stream]

    items = [(stream, pair) for stream in range(n_str) for pair in range(PAIRS)]
    groups = [items[g:g + RWKV_GROUP] for g in range(0, len(items), RWKV_GROUP)]
    _staggered([_rwkv_stages(p_ref, ga_ref, cp_ref, lw_ref, ya_ref, s_scr, carry_scr, n_sub, group)
                for group in groups], RWKV_SKEW)
    for stream in range(n_str):
        carry_scr[stream:stream + 1, :] = p_ref[stream, n_sub * CHUNK - 1:n_sub * CHUNK, :]

    @pl.when(c == pl.num_programs(1) - 1)
    def _():
        for i in range(n_str * PAIRS):
            s = s_scr[i]
            sout_ref[i // PAIRS, 2 * (i % PAIRS)] = s[:HEAD_DIM, :HEAD_DIM]
            sout_ref[i // PAIRS, 2 * (i % PAIRS) + 1] = s[HEAD_DIM:, HEAD_DIM:]


def _rwkv(z3, shift0, wkv0, cparams, lora_w, n_sub, n_str):
    bsz, t_len, _ = z3.shape
    tb = n_sub * CHUNK
    state_spec = pl.BlockSpec((n_str, A_HEADS, HEAD_DIM, HEAD_DIM), lambda b, c: (b, 0, 0, 0))
    return pl.pallas_call(
        functools.partial(_rwkv_kernel, n_sub=n_sub, n_str=n_str),
        grid=(bsz // n_str, t_len // tb),
        in_specs=[pl.BlockSpec((n_str, tb, SHIFT_W), lambda b, c: (b, c, 0)),
                  _at((n_str, tb, D_A), lambda b, c: (b * n_str, c * tb, COL_GA)),
                  pl.BlockSpec((n_str, 1, SHIFT_W), lambda b, c: (b, 0, 0)),
                  state_spec,
                  pl.BlockSpec((_CP_ROWS, D_A), lambda b, c: (0, 0)),
                  pl.BlockSpec((PAIRS, LANES, 2 * LANES), lambda b, c: (0, 0, 0))],
        out_specs=[pl.BlockSpec((n_str, tb, D_A), lambda b, c: (b, c, 0)), state_spec],
        out_shape=[jax.ShapeDtypeStruct((bsz, t_len, D_A), BF16),
                   jax.ShapeDtypeStruct((bsz, A_HEADS, HEAD_DIM, HEAD_DIM), F32)],
        scratch_shapes=[pltpu.VMEM((n_str * PAIRS, LANES, LANES), F32), pltpu.VMEM((8, SHIFT_W), F32)],
        compiler_params=pltpu.CompilerParams(dimension_semantics=("arbitrary", "arbitrary")),
        name="rwkv",
    )(z3, z3, shift0, wkv0, cparams, lora_w)


_SLOPES = tuple(2.0 ** (-8.0 * (h + 1) / Q_HEADS) for h in range(Q_HEADS))


HIST = WINDOW_CHUNKS * CHUNK
HEAD_BATCH = 8


def _attn_kernel(sinks_ref, q_ref, kh_ref, kc_ref, vh_ref, vc_ref, gb_ref, yb_ref, bias_scr, *, masked_history):
    cb = pl.program_id(1)
    n_q = q_ref.shape[1]
    n_k = HIST + n_q

    @pl.when((pl.program_id(0) == 0) & (cb == 0))
    def _():
        qi = lax.broadcasted_iota(jnp.int32, (n_q, n_k), 0)
        kj = lax.broadcasted_iota(jnp.int32, (n_q, n_k), 1)
        dist = jnp.abs(qi + HIST - kj).astype(F32)
        shift = CHUNK.bit_length() - 1
        back = lax.shift_right_logical(qi, shift) + WINDOW_CHUNKS - lax.shift_right_logical(kj, shift)
        visible = (back >= 0) & (back <= WINDOW_CHUNKS)
        for head in range(Q_HEADS):
            bias_scr[head] = jnp.where(visible, -_SLOPES[head] * dist, NEG_INF)
            if masked_history:
                bias_scr[Q_HEADS + head] = jnp.where(visible & (kj >= HIST), -_SLOPES[head] * dist, NEG_INF)

    first = jnp.where(cb == 0, Q_HEADS, 0) if masked_history else 0
    q = q_ref[0] * (HEAD_DIM ** -0.5)
    k_all = jnp.concatenate([kh_ref[0], kc_ref[0]], axis=0)
    v_all = jnp.concatenate([vh_ref[0], vc_ref[0]], axis=0)
    low_k = lax.broadcasted_iota(jnp.int32, (n_k, LANES), 1) < HEAD_DIM
    low_q = lax.broadcasted_iota(jnp.int32, (n_q, LANES), 1) < HEAD_DIM
    ones = jnp.ones((n_k, LANES), BF16)

    def both_halves(pair, head_is_low):
        swapped = pltpu.roll(pair, HEAD_DIM, 1)
        return jnp.where(low_k, pair, swapped) if head_is_low else jnp.where(low_k, swapped, pair)

    k_dup, pv_rhs = [], []
    for h in range(KV_HEADS):
        lanes = slice((h // 2) * LANES, (h // 2 + 1) * LANES)
        k_dup.append(both_halves(k_all[:, lanes], h % 2 == 0).astype(BF16))
        v_dup = both_halves(v_all[:, lanes], h % 2 == 0).astype(BF16)
        pv_rhs.append(jnp.concatenate([v_dup, ones], axis=1))

    def one_head(head):
        pair = q[:, (head // 2) * LANES:(head // 2 + 1) * LANES]
        return jnp.where(low_q, pair, 0.0) if head % 2 == 0 else jnp.where(low_q, 0.0, pair)

    for first_head in range(0, Q_HEADS, HEAD_BATCH):
        batch = range(first_head, first_head + HEAD_BATCH)
        kv_heads = range(first_head // GROUP, (first_head + HEAD_BATCH) // GROUP)
        members = lambda h: range(h * GROUP, (h + 1) * GROUP)
        part = lambda tall, i: tall[(i % GROUP) * n_q:(i % GROUP + 1) * n_q]
        qk = {h: _nt(jnp.concatenate([one_head(i) for i in members(h)], axis=0), k_dup[h]) for h in kv_heads}
        s = {i: part(qk[i // GROUP], i) + bias_scr[first + i] for i in batch}
        m = {i: jnp.maximum(jnp.max(s[i], axis=-1, keepdims=True), sinks_ref[i]) for i in batch}
        e = {i: jnp.exp(s[i] - m[i]).astype(BF16) for i in batch}
        pv_tall = {h: _nn(jnp.concatenate([e[i] for i in members(h)], axis=0), pv_rhs[h]) for h in kv_heads}
        pv = {i: part(pv_tall[i // GROUP], i) for i in batch}
        o = {i: pv[i][:, :LANES] / (pv[i][:, LANES:] + jnp.exp(sinks_ref[i] - m[i])) for i in batch}
        for j in range(first_head // 2, (first_head + HEAD_BATCH) // 2):
            lanes = slice(j * LANES, (j + 1) * LANES)
            gate = gb_ref[0, :, lanes]
            yb_ref[0, :, lanes] = (jnp.where(low_q, o[2 * j], o[2 * j + 1])
                                   * (gate * _sigmoid(gate))).astype(BF16)


def _attn(zr3, sinks, hist_k, hist_v, n_q):
    bsz, t_len, _ = zr3.shape
    masked_history = hist_k is None
    if masked_history:
        assert n_q == HIST
        hist_spec = lambda col: _at((1, HIST, KV_W), lambda b, c: (b, jnp.maximum(c - 1, 0) * HIST, col))
        hist_specs = [hist_spec(COL_K), hist_spec(COL_V)]
        hist_k = hist_v = zr3
    else:
        assert t_len == n_q
        hist_specs = [pl.BlockSpec((1, HIST, KV_W), lambda b, c: (b, 0, 0))] * 2
    cols = lambda width, col: _at((1, n_q, width), lambda b, c: (b, c * n_q, col))
    return pl.pallas_call(
        functools.partial(_attn_kernel, masked_history=masked_history),
        grid=(bsz, t_len // n_q),
        in_specs=[pl.BlockSpec(memory_space=pltpu.SMEM),
                  cols(D_B, COL_Q),
                  hist_specs[0], cols(KV_W, COL_K), hist_specs[1], cols(KV_W, COL_V),
                  cols(D_B, COL_GB)],
        out_specs=pl.BlockSpec((1, n_q, D_B), lambda b, c: (b, c, 0)),
        out_shape=jax.ShapeDtypeStruct((bsz, t_len, D_B), BF16),
        scratch_shapes=[pltpu.VMEM(((2 if masked_history else 1) * Q_HEADS, n_q, HIST + n_q), F32)],
        compiler_params=pltpu.CompilerParams(dimension_semantics=("arbitrary", "arbitrary")),
        name="attn",
    )(sinks, zr3, hist_k, zr3, hist_v, zr3, zr3)


def _merge_kernel(ya_ref, yb_ref, ma_ref, mb_ref, x_ref, pa_ref, pb_ref, wo_ref, gf_ref, o_ref):
    ua = jnp.dot(ya_ref[...], pa_ref[...], preferred_element_type=F32)
    ub = jnp.dot(yb_ref[...], pb_ref[...], preferred_element_type=F32)
    merged = _sigmoid(ma_ref[...]) * ua + _sigmoid(mb_ref[...]) * ub
    o = x_ref[...] + jnp.dot(merged.astype(BF16), wo_ref[...], preferred_element_type=F32)
    o_ref[...] = o * lax.rsqrt(jnp.mean(o * o, axis=-1, keepdims=True) + RMS_EPS) * gf_ref[...]


def _merge(ya, yb, zr, x2d, pa, pb, wo, gf_row, tm):
    m = x2d.shape[0]
    whole = lambda shape: pl.BlockSpec(shape, lambda i: (0, 0), pipeline_mode=pl.Buffered(1))
    return pl.pallas_call(
        _merge_kernel,
        grid=(m // tm,),
        in_specs=[pl.BlockSpec((tm, D_A), lambda i: (i, 0)),
                  pl.BlockSpec((tm, D_B), lambda i: (i, 0)),
                  _at((tm, D_MODEL), lambda i: (i * tm, COL_MA)),
                  _at((tm, D_MODEL), lambda i: (i * tm, COL_MB)),
                  pl.BlockSpec((tm, D_MODEL), lambda i: (i, 0)),
                  whole((D_A, D_MODEL)), whole((D_B, D_MODEL)), whole((D_MODEL, D_MODEL)),
                  whole((1, D_MODEL))],
        out_specs=pl.BlockSpec((tm, D_MODEL), lambda i: (i, 0)),
        out_shape=jax.ShapeDtypeStruct((m, D_MODEL), F32),
        compiler_params=pltpu.CompilerParams(dimension_semantics=("arbitrary",),
                                             vmem_limit_bytes=VMEM_LIMIT),
        name="merge",
    )(ya, yb, zr, zr, x2d, pa, pb, wo, gf_row)


def _layer(x, shift0, wkv0, hist_k, hist_v, wts):
    bsz, t_len, _ = x.shape
    m = bsz * t_len
    x2d = x.reshape(m, D_MODEL)
    zr = _proj(x2d, wts["g_norm"], wts["w_all"], tm=1024, tn=PROJ_TN)
    zr3 = zr.reshape(bsz, t_len, IN_W)
    ya, wkv = _rwkv(zr3, shift0.reshape(bsz, 1, SHIFT_W), wkv0, wts["cparams"], wts["lora_w"], n_sub=1, n_str=4)
    yb = _attn(zr3, wts["sinks"], hist_k, hist_v, n_q=HIST if hist_k is None else t_len)
    y = _merge(ya.reshape(m, D_A), yb.reshape(m, D_B), zr, x2d, wts["p_a"], wts["p_b"], wts["w_o"],
               wts["g_final"], tm=256)
    tail = zr3[:, -min(t_len, HIST):, :]
    k_new = tail[:, :, COL_K:COL_K + KV_W]
    v_new = tail[:, :, COL_V:COL_V + KV_W]
    return y.reshape(bsz, t_len, D_MODEL), wkv, tail[:, -1, :SHIFT_W], k_new, v_new


def _prepare_weights(g_norm, w_in, mu_shift, w0, w_w_up, a0, w_a_up, k_k, k_a, r_k, lnx_w, lnx_b, sinks,
                     p_a, p_b, w_o, g_final):
    assert w_in.shape[1] == IN_W
    w_all = w_in.astype(BF16)
    zeros = jnp.zeros((D_A,), F32)
    cparams = jnp.stack([w0, a0, k_k, k_a, r_k.reshape(D_A), lnx_w, lnx_b, zeros,
                         mu_shift[:D_A], mu_shift[D_A:2 * D_A], mu_shift[2 * D_A:3 * D_A],
                         jnp.tile(mu_shift[3 * D_A:], PAIRS), zeros, zeros, zeros, zeros])
    ww = w_w_up.reshape(LORA, PAIRS, LANES).transpose(1, 0, 2)
    wa = w_a_up.reshape(LORA, PAIRS, LANES).transpose(1, 0, 2)
    z = jnp.zeros_like(ww)
    lora_w = jnp.concatenate([jnp.concatenate([ww, z], axis=2), jnp.concatenate([z, wa], axis=2)],
                             axis=1).astype(BF16)
    return dict(g_norm=g_norm.reshape(1, D_MODEL), w_all=w_all, cparams=cparams,
                lora_w=lora_w, sinks=sinks, p_a=p_a.astype(BF16), p_b=p_b.astype(BF16),
                w_o=w_o.astype(BF16), g_final=g_final.reshape(1, D_MODEL))


def kernel(x_prompt, x_sample, state_wkv, state_shift, cache_k, cache_v, g_norm, w_in, mu_shift, w0, w_w_up,
           a0, w_a_up, k_k, k_a, r_k, lnx_w, lnx_b, sinks, p_a, p_b, w_o, g_final):
    assert g_norm.shape[0] == 1, "single-layer stack"
    wts = _prepare_weights(g_norm[0], w_in[0], mu_shift[0], w0[0], w_w_up[0], a0[0], w_a_up[0], k_k[0],
                           k_a[0], r_k[0], lnx_w[0], lnx_b[0], sinks[0], p_a[0], p_b[0], w_o[0], g_final)
    n_p, n_s = x_prompt.shape[0], x_sample.shape[0]
    cache_win = cache_k.shape[2]
    assert cache_win == WINDOW_CHUNKS * CHUNK

    y_p, wkv_p, shift_p, k_p, v_p = _layer(
        x_prompt, jnp.zeros((n_p, SHIFT_W), F32), jnp.zeros((n_p, A_HEADS, HEAD_DIM, HEAD_DIM), F32),
        None, None, wts)
    hist_k = cache_k[0].reshape(n_s, cache_win, KV_W)
    hist_v = cache_v[0].reshape(n_s, cache_win, KV_W)
    y_s, wkv_s, shift_s, k_s, v_s = _layer(x_sample, state_shift[0], state_wkv[0], hist_k, hist_v, wts)

    rows = lambda u, n: u[:, -cache_win:].reshape(n, cache_win, KV_HEADS, HEAD_DIM)[None]
    k_s = jnp.concatenate([hist_k, k_s], axis=1)
    v_s = jnp.concatenate([hist_v, v_s], axis=1)
    return (y_p, y_s,
            wkv_p[None], shift_p[None], rows(k_p, n_p), rows(v_p, n_p),
            wkv_s[None], shift_s[None], rows(k_s, n_s), rows(v_s, n_s))
```

```python
import functools

import jax
import jax.numpy as jnp
from jax import lax
from jax.experimental import pallas as pl
from jax.experimental.pallas import tpu as pltpu

F32 = jnp.float32
BF16 = jnp.bfloat16

D_MODEL = 2048
HEAD_DIM = 64
CHUNK = 64
D_A = 1024
A_HEADS = 16
LORA = 64
SHIFT_W = 3 * D_A + 2 * LORA
D_B = 1024
Q_HEADS = 16
KV_HEADS = 4
GROUP = 4
KV_W = KV_HEADS * HEAD_DIM
WINDOW_CHUNKS = 2
RMS_EPS = 1e-6
LNX_EPS = 64e-5
NEG_INF = -1e30

LANES = 128
MXU_N = 256
PAIRS = D_A // LANES
COL_GA = SHIFT_W
COL_Q = COL_GA + D_A
COL_K = COL_Q + D_B
COL_V = COL_K + KV_W
COL_GB = COL_V + KV_W
COL_MA = COL_GB + D_B
COL_MB = COL_MA + D_MODEL
IN_W = COL_MB + D_MODEL
PROJ_TN = 2048
VMEM_LIMIT = 56 * 1024 * 1024


def _nn(a, b):
    return jnp.dot(a.astype(BF16), b.astype(BF16), preferred_element_type=F32)


def _nt(a, b):
    return lax.dot_general(a.astype(BF16), b.astype(BF16), (((1,), (1,)), ((), ())),
                           preferred_element_type=F32)


def _sigmoid(x):
    return 1.0 / (1.0 + jnp.exp(-x))


def _at(shape, start):
    return pl.BlockSpec(tuple(pl.Element(n) for n in shape), start)


X_CHUNKS = 4


def _proj_kernel(x_hbm, g_ref, w_ref, o_ref, x_buf, h_scr, x_sem, *, last_cols):
    i, j = pl.program_id(0), pl.program_id(1)
    last = pl.num_programs(1) - 1
    tm = x_buf.shape[1]
    rows = tm // X_CHUNKS

    def x_chunk(tile, c):
        slot = tile % 2
        return pltpu.make_async_copy(x_hbm.at[pl.ds(tile * tm + c * rows, rows), :],
                                     x_buf.at[slot, pl.ds(c * rows, rows), :], x_sem.at[slot])

    @pl.when((i == 0) & (j == 0))
    def _():
        for c in range(X_CHUNKS):
            x_chunk(0, c).start()

    @pl.when((i + 1 < pl.num_programs(0)) & (j < X_CHUNKS))
    def _():
        x_chunk(i + 1, j).start()

    @pl.when(j == 0)
    def _():
        for c in range(X_CHUNKS):
            x_chunk(i, c).wait()
        x = x_buf[i % 2]
        h = x * lax.rsqrt(jnp.mean(x * x, axis=-1, keepdims=True) + RMS_EPS) * g_ref[...]
        h_scr[...] = h.astype(BF16)

    @pl.when(j < last)
    def _():
        o_ref[...] = jnp.dot(h_scr[...], w_ref[...], preferred_element_type=F32)

    @pl.when(j == last)
    def _():
        o_ref[:, :last_cols] = jnp.dot(h_scr[...], w_ref[:, :last_cols], preferred_element_type=F32)


def _proj(x2d, g_row, w, tm, tn):
    m, n = x2d.shape[0], w.shape[1]
    last_cols = -(-(n - (pl.cdiv(n, tn) - 1) * tn) // MXU_N) * MXU_N
    assert pl.cdiv(n, tn) >= X_CHUNKS and m % tm == 0 and tm % (8 * X_CHUNKS) == 0
    return pl.pallas_call(
        functools.partial(_proj_kernel, last_cols=last_cols),
        grid=(m // tm, pl.cdiv(n, tn)),
        in_specs=[pl.BlockSpec(memory_space=pl.ANY),
                  pl.BlockSpec((1, D_MODEL), lambda i, j: (0, 0)),
                  pl.BlockSpec((D_MODEL, tn), lambda i, j: (0, j))],
        out_specs=pl.BlockSpec((tm, tn), lambda i, j: (i, j)),
        out_shape=jax.ShapeDtypeStruct((m, n), F32),
        scratch_shapes=[pltpu.VMEM((2, tm, D_MODEL), F32), pltpu.VMEM((tm, D_MODEL), BF16),
                        pltpu.SemaphoreType.DMA((2,))],
        compiler_params=pltpu.CompilerParams(dimension_semantics=("arbitrary", "arbitrary"),
                                             vmem_limit_bytes=VMEM_LIMIT),
        name="proj",
    )(x2d, g_row, w)


_W0, _A0, _KK, _KA, _RK, _LNW, _LNB, _MU_R, _MU_K, _MU_V, _MU_WA = 0, 1, 2, 3, 4, 5, 6, 8, 9, 10, 11
_CP_ROWS = 16
RWKV_GROUP = 8
RWKV_SKEW = 3


def _rwkv_stages(p_ref, ga_ref, cp_ref, lw_ref, ya_ref, s_scr, carry_scr, n_sub, items):
    def p_rows(stream, rows, off):
        return p_ref[stream, rows, off:off + LANES]

    t_idx = lax.broadcasted_iota(jnp.int32, (CHUNK, LANES), 0)
    lane = lax.broadcasted_iota(jnp.int32, (CHUNK, LANES), 1)
    s_idx = lane & (HEAD_DIM - 1)
    low = lane < HEAD_DIM
    eye = jnp.where(s_idx == t_idx, 1.0, 0.0).astype(F32)
    row4 = lax.broadcasted_iota(jnp.int32, (2 * CHUNK, 2 * LANES), 0)
    col4 = lax.broadcasted_iota(jnp.int32, (2 * CHUNK, 2 * LANES), 1) & (HEAD_DIM - 1)
    causal = ((row4 < CHUNK) & (col4 < row4)) | ((row4 >= CHUNK) & (col4 <= row4 - CHUNK))
    r2 = lax.broadcasted_iota(jnp.int32, (LANES, LANES), 0)
    c2 = lax.broadcasted_iota(jnp.int32, (LANES, LANES), 1)
    same_head = (r2 < HEAD_DIM) == (c2 < HEAD_DIM)
    seg_ones = jnp.where(same_head, 1.0, 0.0).astype(BF16)
    rt = lax.broadcasted_iota(jnp.int32, (CHUNK, CHUNK), 0)
    ct = lax.broadcasted_iota(jnp.int32, (CHUNK, CHUNK), 1)
    tril_ones = jnp.where(ct <= rt, 1.0, 0.0).astype(BF16)

    def stack(q):
        return jnp.concatenate([jnp.where(low, q, 0.0), jnp.where(low, 0.0, q)], axis=0).astype(BF16)

    def seg_sum(xs):
        tall = jnp.concatenate([x.astype(BF16) for x in xs], axis=0)
        out = jnp.dot(tall, seg_ones, preferred_element_type=F32)
        return [out[i * CHUNK:(i + 1) * CHUNK] for i in range(len(xs))]

    wa_off = 3 * D_A
    each = range(len(items))
    stream_of = [stream for stream, _ in items]
    pair_of = [pair for _, pair in items]
    slot_of = [stream_of[i] * PAIRS + pair_of[i] for i in each]
    pair_cols = [slice(pair_of[i] * LANES, (pair_of[i] + 1) * LANES) for i in each]

    for ci in range(n_sub):
        rows = slice(ci * CHUNK, (ci + 1) * CHUNK)
        row = lambda r, i: cp_ref[r:r + 1, pair_cols[i]]

        def shifted(stream, off, mu):
            p = p_rows(stream, rows, off)
            if ci == 0:
                last = carry_scr[stream:stream + 1, off:off + LANES]
            else:
                last = p_rows(stream, slice(ci * CHUNK - 1, ci * CHUNK), off)
            prev = jnp.where(t_idx == 0, last, pltpu.roll(p, 1, 0))
            return p + mu * (prev - p)

        r = [shifted(stream_of[i], pair_of[i] * LANES, row(_MU_R, i)) for i in each]
        k_raw = [shifted(stream_of[i], D_A + pair_of[i] * LANES, row(_MU_K, i)) for i in each]
        v = [shifted(stream_of[i], 2 * D_A + pair_of[i] * LANES, row(_MU_V, i)) for i in each]
        lora_in = {}
        for stream in sorted(set(stream_of)):
            xwa = shifted(stream, wa_off, row(_MU_WA, 0))
            lora_in[stream] = jnp.where(low, jnp.tanh(xwa), xwa).astype(BF16)
        yield
        lora = [_nn(lora_in[stream_of[i]], lw_ref[pair_of[i]]) for i in each]
        yield
        neg = [-(row(_W0, i) + lora[i][:, :LANES]) for i in each]
        softplus = [jnp.maximum(neg[i], 0.0) + jnp.log(1.0 + jnp.exp(-jnp.abs(neg[i]))) for i in each]
        log_decay = [-jnp.exp(-softplus[i] - 0.5) for i in each]
        a_lr = [_sigmoid(row(_A0, i) + lora[i][:, LANES:]) for i in each]
        yield
        kk = [k_raw[i] * row(_KK, i) for i in each]
        kk_sq = seg_sum([kk[i] * kk[i] for i in each])
        yield
        kk = [kk[i] / jnp.maximum(jnp.sqrt(kk_sq[i]), 1e-12) for i in each]
        k = [k_raw[i] * (1.0 + (a_lr[i] - 1.0) * row(_KA, i)) for i in each]
        b_vec = [kk[i] * a_lr[i] for i in each]
        yield
        def prefix_sum(ld):
            hi = ld.astype(BF16)
            res = ld - hi.astype(F32)
            mid = res.astype(BF16)
            lo = (res - mid.astype(F32)).astype(BF16)
            return (jnp.dot(tril_ones, hi, preferred_element_type=F32)
                    + jnp.dot(tril_ones, mid, preferred_element_type=F32)
                    + jnp.dot(tril_ones, lo, preferred_element_type=F32))

        cum = [prefix_sum(log_decay[i]) for i in each]
        yield
        cum_last = [cum[i][CHUNK - 1:CHUNK, :] for i in each]
        g_inv = [jnp.exp(-cum[i]) for i in each]
        g_tail = [jnp.exp(cum_last[i] - cum[i]) for i in each]
        a_t = [-kk[i] * jnp.exp(cum[i] - log_decay[i]) for i in each]
        r_t = [r[i] * jnp.exp(cum[i]) for i in each]
        yield
        stk_b = [stack(b_vec[i] * g_inv[i]) for i in each]
        stk_k = [stack(k[i] * g_inv[i]) for i in each]
        stk_v = [stack(v[i]) for i in each]
        ar = [jnp.concatenate([a_t[i], r_t[i]], axis=0).astype(BF16) for i in each]
        yield
        scores = [jnp.where(causal, _nt(ar[i], jnp.concatenate([stk_b[i], stk_k[i]], axis=0)), 0.0)
                  for i in each]
        a_ab = [scores[i][:CHUNK, :LANES] for i in each]
        a_rb = [scores[i][CHUNK:, :LANES] for i in each]
        a_xk = [scores[i][:, LANES:] for i in each]
        yield
        power = [_nn(a_ab[i], stack(a_ab[i])) for i in each]
        inv = [eye + a_ab[i] for i in each]
        yield
        for _ in range(4):
            both = [_nn(jnp.concatenate([power[i], inv[i]], axis=0), stack(power[i])) for i in each]
            power = [both[i][:CHUNK] for i in each]
            inv = [inv[i] + both[i][CHUNK:] for i in each]
            yield
        inv = [inv[i] + _nn(inv[i], stack(power[i])) for i in each]
        yield
        s_prev = [s_scr[slot_of[i]] for i in each]
        from_state = [_nt(ar[i], s_prev[i]) for i in each]
        from_v = [_nn(a_xk[i], stk_v[i]) for i in each]
        yield
        x = [from_state[i][:CHUNK] + from_v[i][:CHUNK] for i in each]
        u = [_nn(inv[i], stack(x[i])) for i in each]
        yield
        y = [from_state[i][CHUNK:] + from_v[i][CHUNK:] + _nn(a_rb[i], stack(u[i])) for i in each]
        yield
        for i in each:
            uv = jnp.concatenate([u[i], v[i]], axis=0)
            bk = jnp.concatenate([b_vec[i] * g_tail[i], k[i] * g_tail[i]], axis=0)
            s_scr[slot_of[i]] = s_prev[i] * jnp.exp(cum_last[i]) + jnp.where(same_head, _nn(uv.T, bk), 0.0)
        yield
        y_sum = seg_sum(y)
        d = [y[i] - y_sum[i] * (1.0 / HEAD_DIM) for i in each]
        yield
        d_sq = seg_sum([d[i] * d[i] for i in each])
        bonus = seg_sum([r[i] * k[i] * row(_RK, i) for i in each])
        yield
        for i in each:
            var = d_sq[i] * (1.0 / HEAD_DIM)
            yn = d[i] * lax.rsqrt(var + LNX_EPS) * row(_LNW, i) + row(_LNB, i) + bonus[i] * v[i]
            gate = ga_ref[stream_of[i], rows, pair_cols[i]]
            ya_ref[stream_of[i], rows, pair_cols[i]] = (yn * (gate * _sigmoid(gate))).astype(BF16)
        yield


def _staggered(generators, skew):
    pending = list(generators)
    live = []
    rounds = 0
    while pending or live:
        if pending and rounds % skew == 0:
            live.append(pending.pop(0))
        for gen in list(live):
            try:
                next(gen)
            except StopIteration:
                live.remove(gen)
        rounds += 1


def _rwkv_kernel(*refs, n_sub, n_str):
    p_ref, ga_ref, sh_ref, s0_ref, cp_ref, lw_ref, ya_ref, sout_ref, s_scr, carry_scr = refs
    c = pl.program_id(1)
    zero64 = jnp.zeros((HEAD_DIM, HEAD_DIM), F32)

    @pl.when(c == 0)
    def _():
        for stream in range(n_str):
            for pair in range(PAIRS):
                s_scr[stream * PAIRS + pair] = jnp.concatenate(
                    [jnp.concatenate([s0_ref[stream, 2 * pair], zero64], axis=1),
                     jnp.concatenate([zero64, s0_ref[stream, 2 * pair + 1]], axis=1)], axis=0)
            carry_scr[stream:stream + 1, :] = sh_ref[stream]

    items = [(stream, pair) for stream in range(n_str) for pair in range(PAIRS)]
    groups = [items[g:g + RWKV_GROUP] for g in range(0, len(items), RWKV_GROUP)]
    _staggered([_rwkv_stages(p_ref, ga_ref, cp_ref, lw_ref, ya_ref, s_scr, carry_scr, n_sub, group)
                for group in groups], RWKV_SKEW)
    for stream in range(n_str):
        carry_scr[stream:stream + 1, :] = p_ref[stream, n_sub * CHUNK - 1:n_sub * CHUNK, :]

    @pl.when(c == pl.num_programs(1) - 1)
    def _():
        for i in range(n_str * PAIRS):
            s = s_scr[i]
            sout_ref[i // PAIRS, 2 * (i % PAIRS)] = s[:HEAD_DIM, :HEAD_DIM]
            sout_ref[i // PAIRS, 2 * (i % PAIRS) + 1] = s[HEAD_DIM:, HEAD_DIM:]


def _rwkv(z3, shift0, wkv0, cparams, lora_w, n_sub, n_str):
    bsz, t_len, _ = z3.shape
    tb = n_sub * CHUNK
    state_spec = pl.BlockSpec((n_str, A_HEADS, HEAD_DIM, HEAD_DIM), lambda b, c: (b, 0, 0, 0))
    return pl.pallas_call(
        functools.partial(_rwkv_kernel, n_sub=n_sub, n_str=n_str),
        grid=(bsz // n_str, t_len // tb),
        in_specs=[pl.BlockSpec((n_str, tb, SHIFT_W), lambda b, c: (b, c, 0)),
                  _at((n_str, tb, D_A), lambda b, c: (b * n_str, c * tb, COL_GA)),
                  pl.BlockSpec((n_str, 1, SHIFT_W), lambda b, c: (b, 0, 0)),
                  state_spec,
                  pl.BlockSpec((_CP_ROWS, D_A), lambda b, c: (0, 0)),
                  pl.BlockSpec((PAIRS, LANES, 2 * LANES), lambda b, c: (0, 0, 0))],
        out_specs=[pl.BlockSpec((n_str, tb, D_A), lambda b, c: (b, c, 0)), state_spec],
        out_shape=[jax.ShapeDtypeStruct((bsz, t_len, D_A), BF16),
                   jax.ShapeDtypeStruct((bsz, A_HEADS, HEAD_DIM, HEAD_DIM), F32)],
        scratch_shapes=[pltpu.VMEM((n_str * PAIRS, LANES, LANES), F32), pltpu.VMEM((8, SHIFT_W), F32)],
        compiler_params=pltpu.CompilerParams(dimension_semantics=("arbitrary", "arbitrary")),
        name="rwkv",
    )(z3, z3, shift0, wkv0, cparams, lora_w)


_SLOPES = tuple(2.0 ** (-8.0 * (h + 1) / Q_HEADS) for h in range(Q_HEADS))


HIST = WINDOW_CHUNKS * CHUNK
HEAD_BATCH = 8


def _attn_kernel(sinks_ref, q_ref, kh_ref, kc_ref, vh_ref, vc_ref, gb_ref, yb_ref, bias_scr, *, masked_history):
    cb = pl.program_id(1)
    n_q = q_ref.shape[1]
    n_k = HIST + n_q

    @pl.when((pl.program_id(0) == 0) & (cb == 0))
    def _():
        qi = lax.broadcasted_iota(jnp.int32, (n_q, n_k), 0)
        kj = lax.broadcasted_iota(jnp.int32, (n_q, n_k), 1)
        dist = jnp.abs(qi + HIST - kj).astype(F32)
        shift = CHUNK.bit_length() - 1
        back = lax.shift_right_logical(qi, shift) + WINDOW_CHUNKS - lax.shift_right_logical(kj, shift)
        visible = (back >= 0) & (back <= WINDOW_CHUNKS)
        for head in range(Q_HEADS):
            bias_scr[head] = jnp.where(visible, -_SLOPES[head] * dist, NEG_INF)
            if masked_history:
                bias_scr[Q_HEADS + head] = jnp.where(visible & (kj >= HIST), -_SLOPES[head] * dist, NEG_INF)

    first = jnp.where(cb == 0, Q_HEADS, 0) if masked_history else 0
    q = q_ref[0] * (HEAD_DIM ** -0.5)
    k_all = jnp.concatenate([kh_ref[0], kc_ref[0]], axis=0)
    v_all = jnp.concatenate([vh_ref[0], vc_ref[0]], axis=0)
    low_k = lax.broadcasted_iota(jnp.int32, (n_k, LANES), 1) < HEAD_DIM
    low_q = lax.broadcasted_iota(jnp.int32, (n_q, LANES), 1) < HEAD_DIM
    ones = jnp.ones((n_k, LANES), BF16)

    def both_halves(pair, head_is_low):
        swapped = pltpu.roll(pair, HEAD_DIM, 1)
        return jnp.where(low_k, pair, swapped) if head_is_low else jnp.where(low_k, swapped, pair)

    k_dup, pv_rhs = [], []
    for h in range(KV_HEADS):
        lanes = slice((h // 2) * LANES, (h // 2 + 1) * LANES)
        k_dup.append(both_halves(k_all[:, lanes], h % 2 == 0).astype(BF16))
        v_dup = both_halves(v_all[:, lanes], h % 2 == 0).astype(BF16)
        pv_rhs.append(jnp.concatenate([v_dup, ones], axis=1))

    def one_head(head):
        pair = q[:, (head // 2) * LANES:(head // 2 + 1) * LANES]
        return jnp.where(low_q, pair, 0.0) if head % 2 == 0 else jnp.where(low_q, 0.0, pair)

    for first_head in range(0, Q_HEADS, HEAD_BATCH):
        batch = range(first_head, first_head + HEAD_BATCH)
        kv_heads = range(first_head // GROUP, (first_head + HEAD_BATCH) // GROUP)
        members = lambda h: range(h * GROUP, (h + 1) * GROUP)
        part = lambda tall, i: tall[(i % GROUP) * n_q:(i % GROUP + 1) * n_q]
        qk = {h: _nt(jnp.concatenate([one_head(i) for i in members(h)], axis=0), k_dup[h]) for h in kv_heads}
        s = {i: part(qk[i // GROUP], i) + bias_scr[first + i] for i in batch}
        m = {i: jnp.maximum(jnp.max(s[i], axis=-1, keepdims=True), sinks_ref[i]) for i in batch}
        e = {i: jnp.exp(s[i] - m[i]).astype(BF16) for i in batch}
        pv_tall = {h: _nn(jnp.concatenate([e[i] for i in members(h)], axis=0), pv_rhs[h]) for h in kv_heads}
        pv = {i: part(pv_tall[i // GROUP], i) for i in batch}
        o = {i: pv[i][:, :LANES] / (pv[i][:, LANES:] + jnp.exp(sinks_ref[i] - m[i])) for i in batch}
        for j in range(first_head // 2, (first_head + HEAD_BATCH) // 2):
            lanes = slice(j * LANES, (j + 1) * LANES)
            gate = gb_ref[0, :, lanes]
            yb_ref[0, :, lanes] = (jnp.where(low_q, o[2 * j], o[2 * j + 1])
                                   * (gate * _sigmoid(gate))).astype(BF16)


def _attn(zr3, sinks, hist_k, hist_v, n_q):
    bsz, t_len, _ = zr3.shape
    masked_history = hist_k is None
    if masked_history:
        assert n_q == HIST
        hist_spec = lambda col: _at((1, HIST, KV_W), lambda b, c: (b, jnp.maximum(c - 1, 0) * HIST, col))
        hist_specs = [hist_spec(COL_K), hist_spec(COL_V)]
        hist_k = hist_v = zr3
    else:
        assert t_len == n_q
        hist_specs = [pl.BlockSpec((1, HIST, KV_W), lambda b, c: (b, 0, 0))] * 2
    cols = lambda width, col: _at((1, n_q, width), lambda b, c: (b, c * n_q, col))
    return pl.pallas_call(
        functools.partial(_attn_kernel, masked_history=masked_history),
        grid=(bsz, t_len // n_q),
        in_specs=[pl.BlockSpec(memory_space=pltpu.SMEM),
                  cols(D_B, COL_Q),
                  hist_specs[0], cols(KV_W, COL_K), hist_specs[1], cols(KV_W, COL_V),
                  cols(D_B, COL_GB)],
        out_specs=pl.BlockSpec((1, n_q, D_B), lambda b, c: (b, c, 0)),
        out_shape=jax.ShapeDtypeStruct((bsz, t_len, D_B), BF16),
        scratch_shapes=[pltpu.VMEM(((2 if masked_history else 1) * Q_HEADS, n_q, HIST + n_q), F32)],
        compiler_params=pltpu.CompilerParams(dimension_semantics=("arbitrary", "arbitrary")),
        name="attn",
    )(sinks, zr3, hist_k, zr3, hist_v, zr3, zr3)


def _merge_kernel(ya_ref, yb_ref, ma_ref, mb_ref, x_ref, pa_ref, pb_ref, wo_ref, gf_ref, o_ref):
    ua = jnp.dot(ya_ref[...], pa_ref[...], preferred_element_type=F32)
    ub = jnp.dot(yb_ref[...], pb_ref[...], preferred_element_type=F32)
    merged = _sigmoid(ma_ref[...]) * ua + _sigmoid(mb_ref[...]) * ub
    o = x_ref[...] + jnp.dot(merged.astype(BF16), wo_ref[...], preferred_element_type=F32)
    o_ref[...] = o * lax.rsqrt(jnp.mean(o * o, axis=-1, keepdims=True) + RMS_EPS) * gf_ref[...]


def _merge(ya, yb, zr, x2d, pa, pb, wo, gf_row, tm):
    m = x2d.shape[0]
    whole = lambda shape: pl.BlockSpec(shape, lambda i: (0, 0), pipeline_mode=pl.Buffered(1))
    return pl.pallas_call(
        _merge_kernel,
        grid=(m // tm,),
        in_specs=[pl.BlockSpec((tm, D_A), lambda i: (i, 0)),
                  pl.BlockSpec((tm, D_B), lambda i: (i, 0)),
                  _at((tm, D_MODEL), lambda i: (i * tm, COL_MA)),
                  _at((tm, D_MODEL), lambda i: (i * tm, COL_MB)),
                  pl.BlockSpec((tm, D_MODEL), lambda i: (i, 0)),
                  whole((D_A, D_MODEL)), whole((D_B, D_MODEL)), whole((D_MODEL, D_MODEL)),
                  whole((1, D_MODEL))],
        out_specs=pl.BlockSpec((tm, D_MODEL), lambda i: (i, 0)),
        out_shape=jax.ShapeDtypeStruct((m, D_MODEL), F32),
        compiler_params=pltpu.CompilerParams(dimension_semantics=("arbitrary",),
                                             vmem_limit_bytes=VMEM_LIMIT),
        name="merge",
    )(ya, yb, zr, zr, x2d, pa, pb, wo, gf_row)


def _layer(x, shift0, wkv0, hist_k, hist_v, wts):
    bsz, t_len, _ = x.shape
    m = bsz * t_len
    x2d = x.reshape(m, D_MODEL)
    zr = _proj(x2d, wts["g_norm"], wts["w_all"], tm=1024, tn=PROJ_TN)
    zr3 = zr.reshape(bsz, t_len, IN_W)
    ya, wkv = _rwkv(zr3, shift0.reshape(bsz, 1, SHIFT_W), wkv0, wts["cparams"], wts["lora_w"], n_sub=1, n_str=4)
    yb = _attn(zr3, wts["sinks"], hist_k, hist_v, n_q=HIST if hist_k is None else t_len)
    y = _merge(ya.reshape(m, D_A), yb.reshape(m, D_B), zr, x2d, wts["p_a"], wts["p_b"], wts["w_o"],
               wts["g_final"], tm=256)
    tail = zr3[:, -min(t_len, HIST):, :]
    k_new = tail[:, :, COL_K:COL_K + KV_W]
    v_new = tail[:, :, COL_V:COL_V + KV_W]
    return y.reshape(bsz, t_len, D_MODEL), wkv, tail[:, -1, :SHIFT_W], k_new, v_new


def _prepare_weights(g_norm, w_in, mu_shift, w0, w_w_up, a0, w_a_up, k_k, k_a, r_k, lnx_w, lnx_b, sinks,
                     p_a, p_b, w_o, g_final):
    assert w_in.shape[1] == IN_W
    w_all = w_in.astype(BF16)
    zeros = jnp.zeros((D_A,), F32)
    cparams = jnp.stack([w0, a0, k_k, k_a, r_k.reshape(D_A), lnx_w, lnx_b, zeros,
                         mu_shift[:D_A], mu_shift[D_A:2 * D_A], mu_shift[2 * D_A:3 * D_A],
                         jnp.tile(mu_shift[3 * D_A:], PAIRS), zeros, zeros, zeros, zeros])
    ww = w_w_up.reshape(LORA, PAIRS, LANES).transpose(1, 0, 2)
    wa = w_a_up.reshape(LORA, PAIRS, LANES).transpose(1, 0, 2)
    z = jnp.zeros_like(ww)
    lora_w = jnp.concatenate([jnp.concatenate([ww, z], axis=2), jnp.concatenate([z, wa], axis=2)],
                             axis=1).astype(BF16)
    return dict(g_norm=g_norm.reshape(1, D_MODEL), w_all=w_all, cparams=cparams,
                lora_w=lora_w, sinks=sinks, p_a=p_a.astype(BF16), p_b=p_b.astype(BF16),
                w_o=w_o.astype(BF16), g_final=g_final.reshape(1, D_MODEL))


def kernel(x_prompt, x_sample, state_wkv, state_shift, cache_k, cache_v, g_norm, w_in, mu_shift, w0, w_w_up,
           a0, w_a_up, k_k, k_a, r_k, lnx_w, lnx_b, sinks, p_a, p_b, w_o, g_final):
    assert g_norm.shape[0] == 1, "single-layer stack"
    wts = _prepare_weights(g_norm[0], w_in[0], mu_shift[0], w0[0], w_w_up[0], a0[0], w_a_up[0], k_k[0],
                           k_a[0], r_k[0], lnx_w[0], lnx_b[0], sinks[0], p_a[0], p_b[0], w_o[0], g_final)
    n_p, n_s = x_prompt.shape[0], x_sample.shape[0]
    cache_win = cache_k.shape[2]
    assert cache_win == WINDOW_CHUNKS * CHUNK

    y_p, wkv_p, shift_p, k_p, v_p = _layer(
        x_prompt, jnp.zeros((n_p, SHIFT_W), F32), jnp.zeros((n_p, A_HEADS, HEAD_DIM, HEAD_DIM), F32),
        None, None, wts)
    hist_k = cache_k[0].reshape(n_s, cache_win, KV_W)
    hist_v = cache_v[0].reshape(n_s, cache_win, KV_W)
    y_s, wkv_s, shift_s, k_s, v_s = _layer(x_sample, state_shift[0], state_wkv[0], hist_k, hist_v, wts)

    rows = lambda u, n: u[:, -cache_win:].reshape(n, cache_win, KV_HEADS, HEAD_DIM)[None]
    k_s = jnp.concatenate([hist_k, k_s], axis=1)
    v_s = jnp.concatenate([hist_v, v_s], axis=1)
    return (y_p, y_s,
            wkv_p[None], shift_p[None], rows(k_p, n_p), rows(v_p, n_p),
            wkv_s[None], shift_s[None], rows(k_s, n_s), rows(v_s, n_s))
```

```python
import functools

import jax
import jax.numpy as jnp
from jax import lax
from jax.experimental import pallas as pl
from jax.experimental.pallas import tpu as pltpu

F32 = jnp.float32
BF16 = jnp.bfloat16

D_MODEL = 2048
HEAD_DIM = 64
CHUNK = 64
D_A = 1024
A_HEADS = 16
LORA = 64
SHIFT_W = 3 * D_A + 2 * LORA
D_B = 1024
Q_HEADS = 16
KV_HEADS = 4
GROUP = 4
KV_W = KV_HEADS * HEAD_DIM
WINDOW_CHUNKS = 2
RMS_EPS = 1e-6
LNX_EPS = 64e-5
NEG_INF = -1e30

LANES = 128
MXU_N = 256
PAIRS = D_A // LANES
COL_GA = SHIFT_W
COL_Q = COL_GA + D_A
COL_K = COL_Q + D_B
COL_V = COL_K + KV_W
COL_GB = COL_V + KV_W
COL_MA = COL_GB + D_B
COL_MB = COL_MA + D_MODEL
IN_W = COL_MB + D_MODEL
PROJ_TM, PROJ_TN = 1024, 1024
MERGE_TM = 256
RWKV_STREAMS = 4
VMEM_LIMIT = 56 * 1024 * 1024


def _nn(a, b):
    return jnp.dot(a.astype(BF16), b.astype(BF16), preferred_element_type=F32)


def _nt(a, b):
    return lax.dot_general(a.astype(BF16), b.astype(BF16), (((1,), (1,)), ((), ())),
                           preferred_element_type=F32)


def _sigmoid(x):
    return 1.0 / (1.0 + jnp.exp(-x))


def _at(shape, start):
    return pl.BlockSpec(tuple(pl.Element(n) for n in shape), start)


X_CHUNKS = 8


def _proj_kernel(x_hbm, g_ref, w_ref, o_ref, x_buf, h_scr, x_sem, *, last_cols):
    i, j = pl.program_id(0), pl.program_id(1)
    last = pl.num_programs(1) - 1
    tm = x_buf.shape[1]
    rows = tm // X_CHUNKS

    def x_chunk(tile, c):
        slot = tile % 2
        return pltpu.make_async_copy(x_hbm.at[pl.ds(tile * tm + c * rows, rows), :],
                                     x_buf.at[slot, pl.ds(c * rows, rows), :], x_sem.at[slot])

    @pl.when((i == 0) & (j == 0))
    def _():
        for c in range(X_CHUNKS):
            x_chunk(0, c).start()

    @pl.when((i + 1 < pl.num_programs(0)) & (j < X_CHUNKS))
    def _():
        x_chunk(i + 1, j).start()

    @pl.when(j == 0)
    def _():
        for c in range(X_CHUNKS):
            x_chunk(i, c).wait()
        x = x_buf[i % 2]
        h = x * lax.rsqrt(jnp.mean(x * x, axis=-1, keepdims=True) + RMS_EPS) * g_ref[...]
        h_scr[...] = h.astype(BF16)

    @pl.when(j < last)
    def _():
        o_ref[...] = jnp.dot(h_scr[...], w_ref[...], preferred_element_type=F32)

    @pl.when(j == last)
    def _():
        o_ref[:, :last_cols] = jnp.dot(h_scr[...], w_ref[:, :last_cols], preferred_element_type=F32)


def _proj(x2d, g_row, w, tm, tn):
    m, n = x2d.shape[0], w.shape[1]
    last_cols = -(-(n - (pl.cdiv(n, tn) - 1) * tn) // MXU_N) * MXU_N
    assert pl.cdiv(n, tn) >= X_CHUNKS and m % tm == 0 and tm % (8 * X_CHUNKS) == 0
    return pl.pallas_call(
        functools.partial(_proj_kernel, last_cols=last_cols),
        grid=(m // tm, pl.cdiv(n, tn)),
        in_specs=[pl.BlockSpec(memory_space=pl.ANY),
                  pl.BlockSpec((1, D_MODEL), lambda i, j: (0, 0)),
                  pl.BlockSpec((D_MODEL, tn), lambda i, j: (0, j))],
        out_specs=pl.BlockSpec((tm, tn), lambda i, j: (i, j)),
        out_shape=jax.ShapeDtypeStruct((m, n), F32),
        scratch_shapes=[pltpu.VMEM((2, tm, D_MODEL), F32), pltpu.VMEM((tm, D_MODEL), BF16),
                        pltpu.SemaphoreType.DMA((2,))],
        compiler_params=pltpu.CompilerParams(dimension_semantics=("arbitrary", "arbitrary"),
                                             vmem_limit_bytes=VMEM_LIMIT),
        name="proj",
    )(x2d, g_row, w)


_W0, _A0, _KK, _KA, _RK, _LNW, _LNB, _MU_R, _MU_K, _MU_V, _MU_WA = 0, 1, 2, 3, 4, 5, 6, 8, 9, 10, 11
_CP_ROWS = 16
RWKV_GROUP = 8
RWKV_SKEW = 3


def _rwkv_stages(p_ref, ga_ref, cp_ref, lw_ref, ya_ref, s_scr, carry_scr, n_sub, items):
    def p_rows(stream, rows, off):
        return p_ref[stream, rows, off:off + LANES]

    t_idx = lax.broadcasted_iota(jnp.int32, (CHUNK, LANES), 0)
    lane = lax.broadcasted_iota(jnp.int32, (CHUNK, LANES), 1)
    s_idx = lane & (HEAD_DIM - 1)
    low = lane < HEAD_DIM
    eye = jnp.where(s_idx == t_idx, 1.0, 0.0).astype(F32)
    row4 = lax.broadcasted_iota(jnp.int32, (2 * CHUNK, 2 * LANES), 0)
    col4 = lax.broadcasted_iota(jnp.int32, (2 * CHUNK, 2 * LANES), 1) & (HEAD_DIM - 1)
    causal = ((row4 < CHUNK) & (col4 < row4)) | ((row4 >= CHUNK) & (col4 <= row4 - CHUNK))
    r2 = lax.broadcasted_iota(jnp.int32, (LANES, LANES), 0)
    c2 = lax.broadcasted_iota(jnp.int32, (LANES, LANES), 1)
    same_head = (r2 < HEAD_DIM) == (c2 < HEAD_DIM)
    seg_ones = jnp.where(same_head, 1.0, 0.0).astype(BF16)
    rt = lax.broadcasted_iota(jnp.int32, (CHUNK, CHUNK), 0)
    ct = lax.broadcasted_iota(jnp.int32, (CHUNK, CHUNK), 1)
    tril_ones = jnp.where(ct <= rt, 1.0, 0.0).astype(BF16)

    def stack(q):
        return jnp.concatenate([jnp.where(low, q, 0.0), jnp.where(low, 0.0, q)], axis=0).astype(BF16)

    def seg_sum(xs):
        tall = jnp.concatenate([x.astype(BF16) for x in xs], axis=0)
        out = jnp.dot(tall, seg_ones, preferred_element_type=F32)
        return [out[i * CHUNK:(i + 1) * CHUNK] for i in range(len(xs))]

    wa_off = 3 * D_A
    each = range(len(items))
    stream_of = [stream for stream, _ in items]
    pair_of = [pair for _, pair in items]
    slot_of = [stream_of[i] * PAIRS + pair_of[i] for i in each]
    pair_cols = [slice(pair_of[i] * LANES, (pair_of[i] + 1) * LANES) for i in each]

    for ci in range(n_sub):
        rows = slice(ci * CHUNK, (ci + 1) * CHUNK)
        row = lambda r, i: cp_ref[r:r + 1, pair_cols[i]]

        def shifted(stream, off, mu):
            p = p_rows(stream, rows, off)
            if ci == 0:
                last = carry_scr[stream:stream + 1, off:off + LANES]
            else:
                last = p_rows(stream, slice(ci * CHUNK - 1, ci * CHUNK), off)
            prev = jnp.where(t_idx == 0, last, pltpu.roll(p, 1, 0))
            return p + mu * (prev - p)

        r = [shifted(stream_of[i], pair_of[i] * LANES, row(_MU_R, i)) for i in each]
        k_raw = [shifted(stream_of[i], D_A + pair_of[i] * LANES, row(_MU_K, i)) for i in each]
        v = [shifted(stream_of[i], 2 * D_A + pair_of[i] * LANES, row(_MU_V, i)) for i in each]
        lora_in = {}
        for stream in sorted(set(stream_of)):
            xwa = shifted(stream, wa_off, row(_MU_WA, 0))
            lora_in[stream] = jnp.where(low, jnp.tanh(xwa), xwa).astype(BF16)
        yield
        lora = [_nn(lora_in[stream_of[i]], lw_ref[pair_of[i]]) for i in each]
        yield
        neg = [-(row(_W0, i) + lora[i][:, :LANES]) for i in each]
        softplus = [jnp.maximum(neg[i], 0.0) + jnp.log(1.0 + jnp.exp(-jnp.abs(neg[i]))) for i in each]
        log_decay = [-jnp.exp(-softplus[i] - 0.5) for i in each]
        a_lr = [_sigmoid(row(_A0, i) + lora[i][:, LANES:]) for i in each]
        yield
        kk = [k_raw[i] * row(_KK, i) for i in each]
        kk_sq = seg_sum([kk[i] * kk[i] for i in each])
        yield
        kk = [kk[i] / jnp.maximum(jnp.sqrt(kk_sq[i]), 1e-12) for i in each]
        k = [k_raw[i] * (1.0 + (a_lr[i] - 1.0) * row(_KA, i)) for i in each]
        b_vec = [kk[i] * a_lr[i] for i in each]
        yield
        def prefix_sum(ld):
            hi = ld.astype(BF16)
            res = ld - hi.astype(F32)
            mid = res.astype(BF16)
            lo = (res - mid.astype(F32)).astype(BF16)
            return (jnp.dot(tril_ones, hi, preferred_element_type=F32)
                    + jnp.dot(tril_ones, mid, preferred_element_type=F32)
                    + jnp.dot(tril_ones, lo, preferred_element_type=F32))

        cum = [prefix_sum(log_decay[i]) for i in each]
        yield
        cum_last = [cum[i][CHUNK - 1:CHUNK, :] for i in each]
        g_inv = [jnp.exp(-cum[i]) for i in each]
        g_tail = [jnp.exp(cum_last[i] - cum[i]) for i in each]
        a_t = [-kk[i] * jnp.exp(cum[i] - log_decay[i]) for i in each]
        r_t = [r[i] * jnp.exp(cum[i]) for i in each]
        yield
        stk_b = [stack(b_vec[i] * g_inv[i]) for i in each]
        stk_k = [stack(k[i] * g_inv[i]) for i in each]
        stk_v = [stack(v[i]) for i in each]
        ar = [jnp.concatenate([a_t[i], r_t[i]], axis=0).astype(BF16) for i in each]
        yield
        scores = [jnp.where(causal, _nt(ar[i], jnp.concatenate([stk_b[i], stk_k[i]], axis=0)), 0.0)
                  for i in each]
        a_ab = [scores[i][:CHUNK, :LANES] for i in each]
        a_rb = [scores[i][CHUNK:, :LANES] for i in each]
        a_xk = [scores[i][:, LANES:] for i in each]
        yield
        power = [_nn(a_ab[i], stack(a_ab[i])) for i in each]
        inv = [eye + a_ab[i] for i in each]
        yield
        for _ in range(4):
            both = [_nn(jnp.concatenate([power[i], inv[i]], axis=0), stack(power[i])) for i in each]
            power = [both[i][:CHUNK] for i in each]
            inv = [inv[i] + both[i][CHUNK:] for i in each]
            yield
        inv = [inv[i] + _nn(inv[i], stack(power[i])) for i in each]
        yield
        s_prev = [s_scr[slot_of[i]] for i in each]
        from_state = [_nt(ar[i], s_prev[i]) for i in each]
        from_v = [_nn(a_xk[i], stk_v[i]) for i in each]
        yield
        x = [from_state[i][:CHUNK] + from_v[i][:CHUNK] for i in each]
        u = [_nn(inv[i], stack(x[i])) for i in each]
        yield
        y = [from_state[i][CHUNK:] + from_v[i][CHUNK:] + _nn(a_rb[i], stack(u[i])) for i in each]
        yield
        for i in each:
            uv = jnp.concatenate([u[i], v[i]], axis=0)
            bk = jnp.concatenate([b_vec[i] * g_tail[i], k[i] * g_tail[i]], axis=0)
            s_scr[slot_of[i]] = s_prev[i] * jnp.exp(cum_last[i]) + jnp.where(same_head, _nn(uv.T, bk), 0.0)
        yield
        y_sum = seg_sum(y)
        d = [y[i] - y_sum[i] * (1.0 / HEAD_DIM) for i in each]
        yield
        d_sq = seg_sum([d[i] * d[i] for i in each])
        bonus = seg_sum([r[i] * k[i] * row(_RK, i) for i in each])
        yield
        for i in each:
            var = d_sq[i] * (1.0 / HEAD_DIM)
            yn = d[i] * lax.rsqrt(var + LNX_EPS) * row(_LNW, i) + row(_LNB, i) + bonus[i] * v[i]
            gate = ga_ref[stream_of[i], rows, pair_cols[i]]
            ya_ref[stream_of[i], rows, pair_cols[i]] = (yn * (gate * _sigmoid(gate))).astype(BF16)
        yield


def _staggered(generators, skew):
    pending = list(generators)
    live = []
    rounds = 0
    while pending or live:
        if pending and rounds % skew == 0:
            live.append(pending.pop(0))
        for gen in list(live):
            try:
                next(gen)
            except StopIteration:
                live.remove(gen)
        rounds += 1


def _rwkv_kernel(*refs, n_sub, n_str):
    p_ref, ga_ref, sh_ref, s0_ref, cp_ref, lw_ref, ya_ref, sout_ref, s_scr, carry_scr = refs
    c = pl.program_id(1)
    zero64 = jnp.zeros((HEAD_DIM, HEAD_DIM), F32)

    @pl.when(c == 0)
    def _():
        for stream in range(n_str):
            for pair in range(PAIRS):
                s_scr[stream * PAIRS + pair] = jnp.concatenate(
                    [jnp.concatenate([s0_ref[stream, 2 * pair], zero64], axis=1),
                     jnp.concatenate([zero64, s0_ref[stream, 2 * pair + 1]], axis=1)], axis=0)
            carry_scr[stream:stream + 1, :] = sh_ref[stream]

    items = [(stream, pair) for stream in range(n_str) for pair in range(PAIRS)]
    groups = [items[g:g + RWKV_GROUP] for g in range(0, len(items), RWKV_GROUP)]
    _staggered([_rwkv_stages(p_ref, ga_ref, cp_ref, lw_ref, ya_ref, s_scr, carry_scr, n_sub, group)
                for group in groups], RWKV_SKEW)
    for stream in range(n_str):
        carry_scr[stream:stream + 1, :] = p_ref[stream, n_sub * CHUNK - 1:n_sub * CHUNK, :]

    @pl.when(c == pl.num_programs(1) - 1)
    def _():
        for i in range(n_str * PAIRS):
            s = s_scr[i]
            sout_ref[i // PAIRS, 2 * (i % PAIRS)] = s[:HEAD_DIM, :HEAD_DIM]
            sout_ref[i // PAIRS, 2 * (i % PAIRS) + 1] = s[HEAD_DIM:, HEAD_DIM:]


def _rwkv(z3, shift0, wkv0, cparams, lora_w, n_sub, n_str):
    bsz, t_len, _ = z3.shape
    tb = n_sub * CHUNK
    assert bsz % n_str == 0 and t_len % tb == 0 and (n_str * PAIRS) % RWKV_GROUP == 0
    state_spec = pl.BlockSpec((n_str, A_HEADS, HEAD_DIM, HEAD_DIM), lambda b, c: (b, 0, 0, 0))
    return pl.pallas_call(
        functools.partial(_rwkv_kernel, n_sub=n_sub, n_str=n_str),
        grid=(bsz // n_str, t_len // tb),
        in_specs=[pl.BlockSpec((n_str, tb, SHIFT_W), lambda b, c: (b, c, 0)),
                  _at((n_str, tb, D_A), lambda b, c: (b * n_str, c * tb, COL_GA)),
                  pl.BlockSpec((n_str, 1, SHIFT_W), lambda b, c: (b, 0, 0)),
                  state_spec,
                  pl.BlockSpec((_CP_ROWS, D_A), lambda b, c: (0, 0)),
                  pl.BlockSpec((PAIRS, LANES, 2 * LANES), lambda b, c: (0, 0, 0))],
        out_specs=[pl.BlockSpec((n_str, tb, D_A), lambda b, c: (b, c, 0)), state_spec],
        out_shape=[jax.ShapeDtypeStruct((bsz, t_len, D_A), BF16),
                   jax.ShapeDtypeStruct((bsz, A_HEADS, HEAD_DIM, HEAD_DIM), F32)],
        scratch_shapes=[pltpu.VMEM((n_str * PAIRS, LANES, LANES), F32), pltpu.VMEM((8, SHIFT_W), F32)],
        compiler_params=pltpu.CompilerParams(dimension_semantics=("arbitrary", "arbitrary")),
        name="rwkv",
    )(z3, z3, shift0, wkv0, cparams, lora_w)


_SLOPES = tuple(2.0 ** (-8.0 * (h + 1) / Q_HEADS) for h in range(Q_HEADS))


HIST = WINDOW_CHUNKS * CHUNK
HEAD_BATCH = 8


def _attn_kernel(sinks_ref, q_ref, kh_ref, kc_ref, vh_ref, vc_ref, gb_ref, yb_ref, bias_scr, *, masked_history):
    cb = pl.program_id(1)
    n_q = q_ref.shape[1]
    n_k = HIST + n_q

    @pl.when((pl.program_id(0) == 0) & (cb == 0))
    def _():
        qi = lax.broadcasted_iota(jnp.int32, (n_q, n_k), 0)
        kj = lax.broadcasted_iota(jnp.int32, (n_q, n_k), 1)
        dist = jnp.abs(qi + HIST - kj).astype(F32)
        shift = CHUNK.bit_length() - 1
        back = lax.shift_right_logical(qi, shift) + WINDOW_CHUNKS - lax.shift_right_logical(kj, shift)
        visible = (back >= 0) & (back <= WINDOW_CHUNKS)
        for head in range(Q_HEADS):
            bias_scr[head] = jnp.where(visible, -_SLOPES[head] * dist, NEG_INF)
            if masked_history:
                bias_scr[Q_HEADS + head] = jnp.where(visible & (kj >= HIST), -_SLOPES[head] * dist, NEG_INF)

    first = jnp.where(cb == 0, Q_HEADS, 0) if masked_history else 0
    q = q_ref[0] * (HEAD_DIM ** -0.5)
    k_all = jnp.concatenate([kh_ref[0], kc_ref[0]], axis=0)
    v_all = jnp.concatenate([vh_ref[0], vc_ref[0]], axis=0)
    low_k = lax.broadcasted_iota(jnp.int32, (n_k, LANES), 1) < HEAD_DIM
    low_q = lax.broadcasted_iota(jnp.int32, (n_q, LANES), 1) < HEAD_DIM
    ones = jnp.ones((n_k, LANES), BF16)

    def both_halves(pair, head_is_low):
        swapped = pltpu.roll(pair, HEAD_DIM, 1)
        return jnp.where(low_k, pair, swapped) if head_is_low else jnp.where(low_k, swapped, pair)

    k_dup, pv_rhs = [], []
    for h in range(KV_HEADS):
        lanes = slice((h // 2) * LANES, (h // 2 + 1) * LANES)
        k_dup.append(both_halves(k_all[:, lanes], h % 2 == 0).astype(BF16))
        v_dup = both_halves(v_all[:, lanes], h % 2 == 0).astype(BF16)
        pv_rhs.append(jnp.concatenate([v_dup, ones], axis=1))

    def one_head(head):
        pair = q[:, (head // 2) * LANES:(head // 2 + 1) * LANES]
        return jnp.where(low_q, pair, 0.0) if head % 2 == 0 else jnp.where(low_q, 0.0, pair)

    for first_head in range(0, Q_HEADS, HEAD_BATCH):
        batch = range(first_head, first_head + HEAD_BATCH)
        kv_heads = range(first_head // GROUP, (first_head + HEAD_BATCH) // GROUP)
        members = lambda h: range(h * GROUP, (h + 1) * GROUP)
        part = lambda tall, i: tall[(i % GROUP) * n_q:(i % GROUP + 1) * n_q]
        qk = {h: _nt(jnp.concatenate([one_head(i) for i in members(h)], axis=0), k_dup[h]) for h in kv_heads}
        s = {i: part(qk[i // GROUP], i) + bias_scr[first + i] for i in batch}
        m = {i: jnp.maximum(jnp.max(s[i], axis=-1, keepdims=True), sinks_ref[i]) for i in batch}
        e = {i: jnp.exp(s[i] - m[i]).astype(BF16) for i in batch}
        pv_tall = {h: _nn(jnp.concatenate([e[i] for i in members(h)], axis=0), pv_rhs[h]) for h in kv_heads}
        pv = {i: part(pv_tall[i // GROUP], i) for i in batch}
        o = {i: pv[i][:, :LANES] / (pv[i][:, LANES:] + jnp.exp(sinks_ref[i] - m[i])) for i in batch}
        for j in range(first_head // 2, (first_head + HEAD_BATCH) // 2):
            lanes = slice(j * LANES, (j + 1) * LANES)
            gate = gb_ref[0, :, lanes]
            yb_ref[0, :, lanes] = (jnp.where(low_q, o[2 * j], o[2 * j + 1])
                                   * (gate * _sigmoid(gate))).astype(BF16)


def _attn(zr3, sinks, hist_k, hist_v, n_q):
    bsz, t_len, _ = zr3.shape
    masked_history = hist_k is None
    if masked_history:
        assert n_q == HIST
        hist_spec = lambda col: _at((1, HIST, KV_W), lambda b, c: (b, jnp.maximum(c - 1, 0) * HIST, col))
        hist_specs = [hist_spec(COL_K), hist_spec(COL_V)]
        hist_k = hist_v = zr3
    else:
        assert t_len == n_q
        hist_specs = [pl.BlockSpec((1, HIST, KV_W), lambda b, c: (b, 0, 0))] * 2
    assert t_len % n_q == 0 and n_q % CHUNK == 0
    cols = lambda width, col: _at((1, n_q, width), lambda b, c: (b, c * n_q, col))
    return pl.pallas_call(
        functools.partial(_attn_kernel, masked_history=masked_history),
        grid=(bsz, t_len // n_q),
        in_specs=[pl.BlockSpec(memory_space=pltpu.SMEM),
                  cols(D_B, COL_Q),
                  hist_specs[0], cols(KV_W, COL_K), hist_specs[1], cols(KV_W, COL_V),
                  cols(D_B, COL_GB)],
        out_specs=pl.BlockSpec((1, n_q, D_B), lambda b, c: (b, c, 0)),
        out_shape=jax.ShapeDtypeStruct((bsz, t_len, D_B), BF16),
        scratch_shapes=[pltpu.VMEM(((2 if masked_history else 1) * Q_HEADS, n_q, HIST + n_q), F32)],
        compiler_params=pltpu.CompilerParams(dimension_semantics=("arbitrary", "arbitrary")),
        name="attn",
    )(sinks, zr3, hist_k, zr3, hist_v, zr3, zr3)


def _merge_kernel(ya_ref, yb_ref, ma_ref, mb_ref, x_ref, pa_ref, pb_ref, wo_ref, gf_ref, o_ref):
    ua = jnp.dot(ya_ref[...], pa_ref[...], preferred_element_type=F32)
    ub = jnp.dot(yb_ref[...], pb_ref[...], preferred_element_type=F32)
    merged = _sigmoid(ma_ref[...]) * ua + _sigmoid(mb_ref[...]) * ub
    o = x_ref[...] + jnp.dot(merged.astype(BF16), wo_ref[...], preferred_element_type=F32)
    o_ref[...] = o * lax.rsqrt(jnp.mean(o * o, axis=-1, keepdims=True) + RMS_EPS) * gf_ref[...]


def _merge(ya, yb, zr, x2d, pa, pb, wo, gf_row, tm):
    m = x2d.shape[0]
    assert m % tm == 0
    whole = lambda shape: pl.BlockSpec(shape, lambda i: (0, 0), pipeline_mode=pl.Buffered(1))
    return pl.pallas_call(
        _merge_kernel,
        grid=(m // tm,),
        in_specs=[pl.BlockSpec((tm, D_A), lambda i: (i, 0)),
                  pl.BlockSpec((tm, D_B), lambda i: (i, 0)),
                  _at((tm, D_MODEL), lambda i: (i * tm, COL_MA)),
                  _at((tm, D_MODEL), lambda i: (i * tm, COL_MB)),
                  pl.BlockSpec((tm, D_MODEL), lambda i: (i, 0)),
                  whole((D_A, D_MODEL)), whole((D_B, D_MODEL)), whole((D_MODEL, D_MODEL)),
                  whole((1, D_MODEL))],
        out_specs=pl.BlockSpec((tm, D_MODEL), lambda i: (i, 0)),
        out_shape=jax.ShapeDtypeStruct((m, D_MODEL), F32),
        compiler_params=pltpu.CompilerParams(dimension_semantics=("arbitrary",),
                                             vmem_limit_bytes=VMEM_LIMIT),
        name="merge",
    )(ya, yb, zr, zr, x2d, pa, pb, wo, gf_row)


def _layer(x, shift0, wkv0, hist_k, hist_v, wts):
    bsz, t_len, _ = x.shape
    m = bsz * t_len
    x2d = x.reshape(m, D_MODEL)
    zr = _proj(x2d, wts["g_norm"], wts["w_all"], tm=PROJ_TM, tn=PROJ_TN)
    zr3 = zr.reshape(bsz, t_len, IN_W)
    ya, wkv = _rwkv(zr3, shift0.reshape(bsz, 1, SHIFT_W), wkv0, wts["cparams"], wts["lora_w"],
                    n_sub=2 if t_len % (2 * CHUNK) == 0 else 1, n_str=RWKV_STREAMS)
    yb = _attn(zr3, wts["sinks"], hist_k, hist_v, n_q=HIST if hist_k is None else t_len)
    y = _merge(ya.reshape(m, D_A), yb.reshape(m, D_B), zr, x2d, wts["p_a"], wts["p_b"], wts["w_o"],
               wts["g_final"], tm=MERGE_TM)
    tail = zr3[:, -min(t_len, HIST):, :]
    k_new = tail[:, :, COL_K:COL_K + KV_W]
    v_new = tail[:, :, COL_V:COL_V + KV_W]
    return y.reshape(bsz, t_len, D_MODEL), wkv, tail[:, -1, :SHIFT_W], k_new, v_new


def _prepare_weights(g_norm, w_in, mu_shift, w0, w_w_up, a0, w_a_up, k_k, k_a, r_k, lnx_w, lnx_b, sinks,
                     p_a, p_b, w_o, g_final):
    assert w_in.shape[1] == IN_W
    w_all = w_in.astype(BF16)
    zeros = jnp.zeros((D_A,), F32)
    cparams = jnp.stack([w0, a0, k_k, k_a, r_k.reshape(D_A), lnx_w, lnx_b, zeros,
                         mu_shift[:D_A], mu_shift[D_A:2 * D_A], mu_shift[2 * D_A:3 * D_A],
                         jnp.tile(mu_shift[3 * D_A:], PAIRS), zeros, zeros, zeros, zeros])
    ww = w_w_up.reshape(LORA, PAIRS, LANES).transpose(1, 0, 2)
    wa = w_a_up.reshape(LORA, PAIRS, LANES).transpose(1, 0, 2)
    z = jnp.zeros_like(ww)
    lora_w = jnp.concatenate([jnp.concatenate([ww, z], axis=2), jnp.concatenate([z, wa], axis=2)],
                             axis=1).astype(BF16)
    return dict(g_norm=g_norm.reshape(1, D_MODEL), w_all=w_all, cparams=cparams,
                lora_w=lora_w, sinks=sinks, p_a=p_a.astype(BF16), p_b=p_b.astype(BF16),
                w_o=w_o.astype(BF16), g_final=g_final.reshape(1, D_MODEL))


def kernel(x_prompt, x_sample, state_wkv, state_shift, cache_k, cache_v, g_norm, w_in, mu_shift, w0, w_w_up,
           a0, w_a_up, k_k, k_a, r_k, lnx_w, lnx_b, sinks, p_a, p_b, w_o, g_final):
    assert g_norm.shape[0] == 1, "single-layer stack"
    wts = _prepare_weights(g_norm[0], w_in[0], mu_shift[0], w0[0], w_w_up[0], a0[0], w_a_up[0], k_k[0],
                           k_a[0], r_k[0], lnx_w[0], lnx_b[0], sinks[0], p_a[0], p_b[0], w_o[0], g_final)
    n_p, n_s = x_prompt.shape[0], x_sample.shape[0]
    cache_win = cache_k.shape[2]
    assert cache_win == WINDOW_CHUNKS * CHUNK

    y_p, wkv_p, shift_p, k_p, v_p = _layer(
        x_prompt, jnp.zeros((n_p, SHIFT_W), F32), jnp.zeros((n_p, A_HEADS, HEAD_DIM, HEAD_DIM), F32),
        None, None, wts)
    hist_k = cache_k[0].reshape(n_s, cache_win, KV_W)
    hist_v = cache_v[0].reshape(n_s, cache_win, KV_W)
    y_s, wkv_s, shift_s, k_s, v_s = _layer(x_sample, state_shift[0], state_wkv[0], hist_k, hist_v, wts)

    rows = lambda u, n: u[:, -cache_win:].reshape(n, cache_win, KV_HEADS, HEAD_DIM)[None]
    k_s = jnp.concatenate([hist_k, k_s], axis=1)
    v_s = jnp.concatenate([hist_v, v_s], axis=1)
    return (y_p, y_s,
            wkv_p[None], shift_p[None], rows(k_p, n_p), rows(v_p, n_p),
            wkv_s[None], shift_s[None], rows(k_s, n_s), rows(v_s, n_s))
```
